```python
import math
import jax, jax.numpy as jnp
from jax import lax
import numpy as np

D_MODEL = 1024
BATCH = 8
SEQ = 2048
DEPTH = 4
DEC_BATCH = 32
DEC_SEQ = 4
PAST_LEN = 8192
PAGE_SIZE = 128

N_MIXERS = 4
POOL_WINDOWS = (2, 4, 8, 16)
POOL_GROUPS = len(POOL_WINDOWS)
POOL_GD = D_MODEL // POOL_GROUPS
POOL_BUF = max(POOL_WINDOWS) - 1
DIFF_HEADS = 8
DIFF_HD = D_MODEL // (2 * DIFF_HEADS)
DIFF_VD = 2 * DIFF_HD
FOX_HEADS = 16
FOX_HD = D_MODEL // FOX_HEADS
FOX_GATE_BIAS = 3.0
SG_WIDTH = 2 * D_MODEL
SG_GROUPS = 8
SG_GD = SG_WIDTH // SG_GROUPS
SG_CHUNK = 128
FFN_DIM = ((8 * D_MODEL // 3 + 127) // 128) * 128
FFN_CONV = 3
ROPE_THETA = 500000.0
ROPE_DIM = DIFF_HD // 4
Q_BLOCK = 128
NORM_EPS = 1e-6

kernel_name = "interleaved_pool_diffattn_fox_sgmlp_convffn_step"


def _rms(x, g):
    xf = x.astype(jnp.float32)
    y = xf * lax.rsqrt(jnp.mean(xf * xf, axis=-1, keepdims=True) + NORM_EPS)
    return (y * g.astype(jnp.float32)).astype(x.dtype)


def _rope(x, pos):
    half = ROPE_DIM // 2
    inv = ROPE_THETA ** (-jnp.arange(half, dtype=jnp.float32) / half)
    ang = pos.astype(jnp.float32)[:, None] * inv[None, :]
    cos = jnp.cos(ang)[None, :, None, :]
    sin = jnp.sin(ang)[None, :, None, :]
    xr = x[..., :ROPE_DIM].astype(jnp.float32)
    x1, x2 = xr[..., :half], xr[..., half:]
    rot = jnp.concatenate([x1 * cos - x2 * sin, x2 * cos + x1 * sin], axis=-1).astype(x.dtype)
    return jnp.concatenate([rot, x[..., ROPE_DIM:]], axis=-1)


def _gather_pages(cache, page_table):
    g = cache[page_table]
    return g.reshape(g.shape[0], g.shape[1] * g.shape[2], *g.shape[3:])


def _logits(q, k, q_pos, k_pos, scale):
    s = jnp.einsum("bqhd,bkhd->bhqk", q, k).astype(jnp.float32) * scale
    return jnp.where(q_pos[:, None] >= k_pos[None, :], s, -jnp.inf)


def _weighted_values(a, vs):
    out = None
    off = 0
    for v in vs:
        n = v.shape[1]
        o = jnp.einsum("bhqk,bkhd->bqhd", a[..., off:off + n].astype(v.dtype), v)
        out = o if out is None else out + o
        off += n
    return out


def _sweep_queries(core, q_args, q_pos):
    T = q_pos.shape[0]
    if T <= Q_BLOCK:
        return core(q_args, q_pos)
    nb = T // Q_BLOCK

    def split(a):
        return jnp.moveaxis(a.reshape(a.shape[0], nb, Q_BLOCK, *a.shape[2:]), 1, 0)

    out = lax.map(lambda xs: core(xs[0], xs[1]),
                  (tuple(split(a) for a in q_args), q_pos.reshape(nb, Q_BLOCK)))
    out = jnp.moveaxis(out, 0, 1)
    return out.reshape(out.shape[0], T, *out.shape[3:])


def _pool_mixer(h, prev, start, w_pool, pool_scale):
    B, T, _ = h.shape
    xcat = jnp.concatenate([prev, h], axis=1)
    cs = jnp.concatenate([jnp.zeros((B, 1, D_MODEL), jnp.float32),
                          jnp.cumsum(xcat.astype(jnp.float32), axis=1)], axis=1)
    pos = start + jnp.arange(T, dtype=jnp.int32)
    hi = cs[:, POOL_BUF + 1:POOL_BUF + 1 + T]
    outs = []
    for g, win in enumerate(POOL_WINDOWS):
        sl = slice(g * POOL_GD, (g + 1) * POOL_GD)
        lo = cs[:, POOL_BUF + 1 - win:POOL_BUF + 1 - win + T, sl]
        cnt = jnp.minimum(pos + 1, win).astype(jnp.float32)[None, :, None]
        d = ((hi[..., sl] - lo) / cnt - h[..., sl].astype(jnp.float32)).astype(h.dtype)
        outs.append(d @ w_pool[g])
    y = jnp.concatenate(outs, axis=-1) * pool_scale
    return y, xcat[:, -POOL_BUF:]


def _diff_mixer(h, start, past_k, past_v, w_qkv, q_norm, k_norm, lq1, lk1, lq2, lk2,
                o_norm, w_o, lambda_init):
    B, T, _ = h.shape
    pos = start + jnp.arange(T, dtype=jnp.int32)
    q, k, v = jnp.split(h @ w_qkv, 3, axis=-1)
    q = _rms(q.reshape(B, T, DIFF_HEADS, 2, DIFF_HD), q_norm)
    k = _rms(k.reshape(B, T, DIFF_HEADS, 2, DIFF_HD), k_norm)
    v = v.reshape(B, T, DIFF_HEADS, DIFF_VD)
    q1 = _rope(q[..., 0, :], pos)
    q2 = _rope(q[..., 1, :], pos)
    k_new = jnp.concatenate([_rope(k[..., 0, :], pos), _rope(k[..., 1, :], pos)], axis=-1)
    segs = [(k_new, v, pos)]
    if past_k is not None:
        segs = [(past_k, past_v, jnp.arange(start, dtype=jnp.int32))] + segs
    lam = (jnp.exp(jnp.sum(lq1.astype(jnp.float32) * lk1.astype(jnp.float32)))
           - jnp.exp(jnp.sum(lq2.astype(jnp.float32) * lk2.astype(jnp.float32))) + lambda_init)
    scale = DIFF_HD ** -0.5

    def core(qs, qp):
        qb1, qb2 = qs
        s1 = jnp.concatenate([_logits(qb1, kk[..., :DIFF_HD], qp, kp, scale) for kk, _, kp in segs], axis=-1)
        s2 = jnp.concatenate([_logits(qb2, kk[..., DIFF_HD:], qp, kp, scale) for kk, _, kp in segs], axis=-1)
        a = jax.nn.softmax(s1, axis=-1) - lam * jax.nn.softmax(s2, axis=-1)
        return _weighted_values(a, [vv for _, vv, _ in segs])

    o = _sweep_queries(core, (q1, q2), pos)
    o = _rms(o, o_norm) * (1.0 - lambda_init)
    return o.reshape(B, T, DIFF_HEADS * DIFF_VD) @ w_o, k_new, v


def _fox_mixer(h, start, past_k, past_v, past_logf, w_qkvf, b_f, q_norm, k_norm, w_o):
    B, T, _ = h.shape
    pos = start + jnp.arange(T, dtype=jnp.int32)
    proj = h @ w_qkvf
    q = _rms(proj[..., :D_MODEL].reshape(B, T, FOX_HEADS, FOX_HD), q_norm)
    k = _rms(proj[..., D_MODEL:2 * D_MODEL].reshape(B, T, FOX_HEADS, FOX_HD), k_norm)
    v = proj[..., 2 * D_MODEL:3 * D_MODEL].reshape(B, T, FOX_HEADS, FOX_HD)
    logf = jax.nn.log_sigmoid((proj[..., 3 * D_MODEL:] + b_f).astype(jnp.float32))
    c_new = jnp.cumsum(logf, axis=1)
    segs = []
    if past_k is not None:
        c_past = jnp.cumsum(past_logf.astype(jnp.float32), axis=1)
        c_new = c_new + c_past[:, -1:]
        segs.append((past_k, past_v, jnp.swapaxes(c_past, 1, 2), jnp.arange(start, dtype=jnp.int32)))
    segs.append((k, v, jnp.swapaxes(c_new, 1, 2), pos))
    scale = FOX_HD ** -0.5

    def core(qs, qp):
        qb, cq = qs
        cqT = jnp.swapaxes(cq, 1, 2)[..., None]
        s = jnp.concatenate([_logits(qb, kk, qp, kp, scale) + cqT - ck[:, :, None, :]
                             for kk, _, ck, kp in segs], axis=-1)
        return _weighted_values(jax.nn.softmax(s, axis=-1), [vv for _, vv, _, _ in segs])

    o = _sweep_queries(core, (q, c_new), pos)
    return o.reshape(B, T, D_MODEL) @ w_o, k, v, logf.astype(h.dtype)


def _sgu_mixer(h, w_in, v_norm, w_s, b_s, w_out):
    B, T, _ = h.shape
    u, v = jnp.split(jax.nn.gelu(h @ w_in, approximate=False), 2, axis=-1)
    v = _rms(v, v_norm)
    L = min(T, SG_CHUNK)
    nc = T // L
    tri = jnp.tril(jnp.ones((L, L), dtype=bool))
    ws = jnp.where(tri[None], w_s[:, :L, :L], 0.0)
    vc = v.reshape(B, nc, L, SG_GROUPS, SG_GD)
    mixed = jnp.einsum("gts,bcsgd->bctgd", ws, vc) + b_s[:, :L].T[None, None, :, :, None]
    y = (u * mixed.reshape(B, T, SG_WIDTH)) @ w_out
    return y, v


def _conv_ffn(h, prev, w_up, conv_w, conv_b, w_down):
    T = h.shape[1]
    xp = jnp.concatenate([prev, h @ w_up], axis=1)
    acc = conv_b
    for j in range(FFN_CONV):
        acc = acc + conv_w[j] * xp[:, j:j + T]
    gate, val = jnp.split(acc, 2, axis=-1)
    return (jax.nn.silu(gate) * val) @ w_down, xp[:, -(FFN_CONV - 1):]


def _lambda_init(layer_idx):
    return 0.8 - 0.6 * math.exp(-0.3 * layer_idx)


def _trunk(x, start, pool_prev, diff_past, fox_past, conv_prev, prm):
    new = {}
    conv_new = []
    for i in range(DEPTH):
        h = _rms(x, prm["norm_mix"][i])
        kind = i % N_MIXERS
        if kind == 0:
            y, new["pool"] = _pool_mixer(h, pool_prev, start, prm["pool_w"], prm["pool_scale"])
        elif kind == 1:
            y, new["diff_k"], new["diff_v"] = _diff_mixer(
                h, start, diff_past[0], diff_past[1], prm["diff_w_qkv"], prm["diff_q_norm"],
                prm["diff_k_norm"], prm["diff_lq1"], prm["diff_lk1"], prm["diff_lq2"], prm["diff_lk2"],
                prm["diff_o_norm"], prm["diff_w_o"], _lambda_init(i))
        elif kind == 2:
            y, new["fox_k"], new["fox_v"], new["fox_logf"] = _fox_mixer(
                h, start, fox_past[0], fox_past[1], fox_past[2], prm["fox_w_qkvf"], prm["fox_b_f"],
                prm["fox_q_norm"], prm["fox_k_norm"], prm["fox_w_o"])
        else:
            y, new["sgu_v"] = _sgu_mixer(h, prm["sgu_w_in"], prm["sgu_v_norm"], prm["sgu_w_s"],
                                         prm["sgu_b_s"], prm["sgu_w_out"])
        x = x + y
        h = _rms(x, prm["norm_ffn"][i])
        y, c_state = _conv_ffn(h, conv_prev[i], prm["ffn_w_up"][i], prm["ffn_conv_w"][i],
                               prm["ffn_conv_b"][i], prm["ffn_w_down"][i])
        conv_new.append(c_state)
        x = x + y
    return x, new, jnp.stack(conv_new, axis=0)


def setup_inputs(seed: int = 0) -> dict:
    key = jax.random.key(seed)
    ks = jax.random.split(key, 40)
    f32 = jnp.float32

    def nrm(k, shape, scale):
        return jax.random.normal(k, shape, f32) * scale

    def gain(k, shape):
        return 1.0 + nrm(k, shape, 0.02)

    n_pages = PAST_LEN // PAGE_SIZE
    n_pool = (DEC_BATCH * n_pages * 5) // 4
    page_table = jax.random.permutation(ks[0], n_pool)[:DEC_BATCH * n_pages].reshape(
        DEC_BATCH, n_pages).astype(jnp.int32)
    return {
        "x_prompt": nrm(ks[1], (BATCH, SEQ, D_MODEL), 1.0),
        "x_sample": nrm(ks[2], (DEC_BATCH, DEC_SEQ, D_MODEL), 1.0),
        "state_pool": nrm(ks[3], (DEC_BATCH, POOL_BUF, D_MODEL), 1.0),
        "cache_diff_k": nrm(ks[4], (n_pool, PAGE_SIZE, DIFF_HEADS, 2 * DIFF_HD), 1.0),
        "cache_diff_v": nrm(ks[5], (n_pool, PAGE_SIZE, DIFF_HEADS, DIFF_VD), 1.0),
        "cache_fox_k": nrm(ks[6], (n_pool, PAGE_SIZE, FOX_HEADS, FOX_HD), 1.0),
        "cache_fox_v": nrm(ks[7], (n_pool, PAGE_SIZE, FOX_HEADS, FOX_HD), 1.0),
        "cache_fox_logf": jax.nn.log_sigmoid(FOX_GATE_BIAS + nrm(ks[8], (n_pool, PAGE_SIZE, FOX_HEADS), 1.0)),
        "state_ffn_conv": nrm(ks[9], (DEPTH, DEC_BATCH, FFN_CONV - 1, 2 * FFN_DIM), 1.0),
        "page_table": page_table,
        "norm_mix": gain(ks[10], (DEPTH, D_MODEL)),
        "norm_ffn": gain(ks[11], (DEPTH, D_MODEL)),
        "pool_w": nrm(ks[12], (POOL_GROUPS, POOL_GD, POOL_GD), POOL_GD ** -0.5),
        "pool_scale": gain(ks[13], (D_MODEL,)),
        "diff_w_qkv": nrm(ks[14], (D_MODEL, 3 * DIFF_HEADS * DIFF_VD), D_MODEL ** -0.5),
        "diff_q_norm": gain(ks[15], (DIFF_HD,)),
        "diff_k_norm": gain(ks[16], (DIFF_HD,)),
        "diff_lq1": nrm(ks[17], (DIFF_HD,), 0.1),
        "diff_lk1": nrm(ks[18], (DIFF_HD,), 0.1),
        "diff_lq2": nrm(ks[19], (DIFF_HD,), 0.1),
        "diff_lk2": nrm(ks[20], (DIFF_HD,), 0.1),
        "diff_o_norm": gain(ks[21], (DIFF_VD,)),
        "diff_w_o": nrm(ks[22], (DIFF_HEADS * DIFF_VD, D_MODEL), (DIFF_HEADS * DIFF_VD) ** -0.5),
        "fox_w_qkvf": nrm(ks[23], (D_MODEL, 3 * D_MODEL + FOX_HEADS), D_MODEL ** -0.5),
        "fox_b_f": FOX_GATE_BIAS + nrm(ks[24], (FOX_HEADS,), 0.5),
        "fox_q_norm": gain(ks[25], (FOX_HD,)),
        "fox_k_norm": gain(ks[26], (FOX_HD,)),
        "fox_w_o": nrm(ks[27], (D_MODEL, D_MODEL), D_MODEL ** -0.5),
        "sgu_w_in": nrm(ks[28], (D_MODEL, 2 * SG_WIDTH), D_MODEL ** -0.5),
        "sgu_v_norm": gain(ks[29], (SG_WIDTH,)),
        "sgu_w_s": nrm(ks[30], (SG_GROUPS, SG_CHUNK, SG_CHUNK), SG_CHUNK ** -0.5),
        "sgu_b_s": gain(ks[31], (SG_GROUPS, SG_CHUNK)),
        "sgu_w_out": nrm(ks[32], (SG_WIDTH, D_MODEL), SG_WIDTH ** -0.5),
        "ffn_w_up": nrm(ks[33], (DEPTH, D_MODEL, 2 * FFN_DIM), D_MODEL ** -0.5),
        "ffn_conv_w": nrm(ks[34], (DEPTH, FFN_CONV, 2 * FFN_DIM), FFN_CONV ** -0.5),
        "ffn_conv_b": nrm(ks[35], (DEPTH, 2 * FFN_DIM), 0.02),
        "ffn_w_down": nrm(ks[36], (DEPTH, FFN_DIM, D_MODEL), FFN_DIM ** -0.5),
    }


def reference(x_prompt, x_sample, state_pool, cache_diff_k, cache_diff_v, cache_fox_k, cache_fox_v,
              cache_fox_logf, state_ffn_conv, page_table, norm_mix, norm_ffn, pool_w, pool_scale,
              diff_w_qkv, diff_q_norm, diff_k_norm, diff_lq1, diff_lk1, diff_lq2, diff_lk2, diff_o_norm,
              diff_w_o, fox_w_qkvf, fox_b_f, fox_q_norm, fox_k_norm, fox_w_o, sgu_w_in, sgu_v_norm,
              sgu_w_s, sgu_b_s, sgu_w_out, ffn_w_up, ffn_conv_w, ffn_conv_b, ffn_w_down):
    prm = {
        "norm_mix": norm_mix, "norm_ffn": norm_ffn, "pool_w": pool_w, "pool_scale": pool_scale,
        "diff_w_qkv": diff_w_qkv, "diff_q_norm": diff_q_norm, "diff_k_norm": diff_k_norm,
        "diff_lq1": diff_lq1, "diff_lk1": diff_lk1, "diff_lq2": diff_lq2, "diff_lk2": diff_lk2,
        "diff_o_norm": diff_o_norm, "diff_w_o": diff_w_o, "fox_w_qkvf": fox_w_qkvf, "fox_b_f": fox_b_f,
        "fox_q_norm": fox_q_norm, "fox_k_norm": fox_k_norm, "fox_w_o": fox_w_o, "sgu_w_in": sgu_w_in,
        "sgu_v_norm": sgu_v_norm, "sgu_w_s": sgu_w_s, "sgu_b_s": sgu_b_s, "sgu_w_out": sgu_w_out,
        "ffn_w_up": ffn_w_up, "ffn_conv_w": ffn_conv_w, "ffn_conv_b": ffn_conv_b, "ffn_w_down": ffn_w_down,
    }
    bp = x_prompt.shape[0]
    y_prompt, sp, conv_p = _trunk(
        x_prompt, 0, jnp.zeros((bp, POOL_BUF, D_MODEL), x_prompt.dtype), (None, None), (None, None, None),
        jnp.zeros((DEPTH, bp, FFN_CONV - 1, 2 * FFN_DIM), x_prompt.dtype), prm)
    past_len = page_table.shape[1] * PAGE_SIZE
    diff_past = (_gather_pages(cache_diff_k, page_table), _gather_pages(cache_diff_v, page_table))
    fox_past = (_gather_pages(cache_fox_k, page_table), _gather_pages(cache_fox_v, page_table),
                _gather_pages(cache_fox_logf, page_table))
    y_sample, ss, conv_s = _trunk(x_sample, past_len, state_pool, diff_past, fox_past, state_ffn_conv, prm)
    return (y_prompt, y_sample, sp["pool"], ss["pool"], sp["diff_k"], sp["diff_v"], ss["diff_k"], ss["diff_v"],
            sp["fox_k"], sp["fox_v"], sp["fox_logf"], ss["fox_k"], ss["fox_v"], ss["fox_logf"], ss["sgu_v"],
            conv_p, conv_s)
```

```python
import functools
import math

import jax
import jax.numpy as jnp
from jax import lax
from jax.experimental import pallas as pl
from jax.experimental.pallas import tpu as pltpu

F32 = jnp.float32
BF16 = jnp.bfloat16

D_MODEL = 1024
DEPTH = 4
PAGE_SIZE = 128
POOL_WINDOWS = (2, 4, 8, 16)
POOL_GD = D_MODEL // len(POOL_WINDOWS)
POOL_BUF = max(POOL_WINDOWS) - 1
DIFF_HEADS = 8
DIFF_HD = 64
FOX_HEADS = 16
FOX_HD = 64
SG_WIDTH = 2 * D_MODEL
SG_GROUPS = 8
SG_GD = SG_WIDTH // SG_GROUPS
SG_CHUNK = 128
FFN_DIM = 2816
FFN_CONV = 3
ROPE_THETA = 500000.0
ROPE_DIM = DIFF_HD // 4
NORM_EPS = 1e-6

LANES = 128
SUBLANES = 8
FFN_CK = 256
FFN_NCH = FFN_DIM // FFN_CK
SEG = 64
NSEG = D_MODEL // SEG
PAGES_PER_STEP = 8
ROW_TILE = 512
SGU_TILE = 256
ATT_TILE = 256
VMEM_LIMIT = 56 * 1024 * 1024


def _cparams(*sem):
    return pltpu.CompilerParams(dimension_semantics=sem, vmem_limit_bytes=VMEM_LIMIT)


def _rms_rows(x, g):
    ms = jnp.mean(x * x, axis=-1, keepdims=True)
    return x * lax.rsqrt(ms + NORM_EPS) * g


def _dot(a, b):
    return jnp.dot(a, b, preferred_element_type=F32)


def _dot_nt(a, b):
    return lax.dot_general(a, b, (((1,), (1,)), ((), ())), preferred_element_type=F32)


def _dot_tn(a, b):
    return lax.dot_general(a, b, (((0,), (0,)), ((), ())), preferred_element_type=F32)


def _split3(a):
    hi = a.astype(BF16)
    r1 = a - hi.astype(F32)
    mid = r1.astype(BF16)
    lo = (r1 - mid.astype(F32)).astype(BF16)
    return hi, mid, lo


def _time_mask(n, shift):
    r = lax.broadcasted_iota(jnp.int32, (n, n), 0)
    c = lax.broadcasted_iota(jnp.int32, (n, n), 1)
    m = c <= r
    if shift > 1:
        m = jnp.logical_and(m, (r - c) % shift == 0)
    return m


def _const_spec(shape):
    nd = len(shape)
    return pl.BlockSpec(shape, lambda *_: (0,) * nd, pipeline_mode=pl.Buffered(1))


def _pool_kernel(x_ref, prev_ref, g_ref, w_ref, scale_ref, y_ref, st_ref, hext,
                 *, tq, shift, halo, start, nt):
    t = pl.program_id(1)

    @pl.when(t == 0)
    def _():
        hext[0:halo, :] = prev_ref[0]

    x = x_ref[0]
    h = _rms_rows(x, g_ref[...])
    hext[halo:halo + tq, :] = h
    row = lax.broadcasted_iota(jnp.int32, (tq, 1), 0)
    pos = start + (t * tq + row) // shift
    for g, win in enumerate(POOL_WINDOWS):
        c0 = g * POOL_GD
        hg = h[:, c0:c0 + POOL_GD]
        s = hg
        for j in range(1, win):
            o = halo - j * shift
            s = s + hext[o:o + tq, c0:c0 + POOL_GD]
        cnt = jnp.minimum(pos + 1, win).astype(F32)
        d = (s / cnt - hg).astype(BF16)
        y = _dot(d, w_ref[g])
        y_ref[0, :, c0:c0 + POOL_GD] = x[:, c0:c0 + POOL_GD] + y * scale_ref[:, c0:c0 + POOL_GD]

    n = POOL_BUF * shift

    @pl.when(t == nt - 1)
    def _():
        st_ref[0] = hext[halo + tq - n:halo + tq, :]

    if nt > 1:
        hext[0:halo, :] = hext[tq:tq + halo, :]


def _pool_layer(x, prev, g, w_bf, scale, *, shift, start):
    B, T, D = x.shape
    tq = min(T, ROW_TILE)
    nt = T // tq
    halo = prev.shape[1]
    n = POOL_BUF * shift
    kern = functools.partial(_pool_kernel, tq=tq, shift=shift, halo=halo, start=start, nt=nt)
    return pl.pallas_call(
        kern,
        grid=(B, nt),
        in_specs=[
            pl.BlockSpec((1, tq, D), lambda b, t: (b, t, 0)),
            pl.BlockSpec((1, halo, D), lambda b, t: (b, 0, 0)),
            _const_spec((1, D)),
            _const_spec(w_bf.shape),
            _const_spec((1, D)),
        ],
        out_specs=[
            pl.BlockSpec((1, tq, D), lambda b, t: (b, t, 0)),
            pl.BlockSpec((1, n, D), lambda b, t: (b, 0, 0)),
        ],
        out_shape=[jax.ShapeDtypeStruct((B, T, D), F32), jax.ShapeDtypeStruct((B, n, D), F32)],
        scratch_shapes=[pltpu.VMEM((halo + tq, D), F32)],
        compiler_params=_cparams("arbitrary", "arbitrary"),
        name="pool_mixer",
    )(x, prev, g, w_bf, scale)


def _ffn_kernel(x_ref, prev_ref, g_ref, wu_ref, cw_ref, cb_ref, wd_ref, y_ref, st_ref,
                hb, buf, carry, *, tq, shift, halo, nt):
    t = pl.program_id(1)
    x = x_ref[0]
    hb[...] = _rms_rows(x, g_ref[...]).astype(BF16)
    y_ref[0] = x

    @pl.when(t == 0)
    def _():
        carry[...] = prev_ref[0]

    def conv(u, idx, slot):
        buf[slot, 0:halo, :] = carry[idx]
        buf[slot, halo:halo + tq, :] = u
        carry[idx] = buf[slot, tq:tq + halo, :]
        w = cw_ref[idx]
        o2 = halo - 2 * shift
        o1 = halo - shift
        return (cb_ref[idx] + w[0:1] * buf[slot, o2:o2 + tq, :]
                + w[1:2] * buf[slot, o1:o1 + tq, :] + w[2:3] * u)

    def chunk(c, _):
        hv = hb[...]
        ag = conv(_dot(hv, wu_ref[c]), c, 0)
        av = conv(_dot(hv, wu_ref[FFN_NCH + c]), FFN_NCH + c, 1)
        act = (ag / (1.0 + jnp.exp(-ag)) * av).astype(BF16)
        y_ref[0] += _dot(act, wd_ref[c])
        return 0

    lax.fori_loop(0, FFN_NCH, chunk, 0)

    @pl.when(t == nt - 1)
    def _():
        st_ref[0] = carry[:, halo - 2 * shift:halo, :]


def _ffn_layer(x, prev, g, wu, cw, cb, wd, *, shift):
    B, T, D = x.shape
    tq = min(T, ROW_TILE)
    nt = T // tq
    halo = prev.shape[2]
    ns = (FFN_CONV - 1) * shift
    kern = functools.partial(_ffn_kernel, tq=tq, shift=shift, halo=halo, nt=nt)
    return pl.pallas_call(
        kern,
        grid=(B, nt),
        in_specs=[
            pl.BlockSpec((1, tq, D), lambda b, t: (b, t, 0)),
            pl.BlockSpec((1, 2 * FFN_NCH, halo, FFN_CK), lambda b, t: (b, 0, 0, 0)),
            _const_spec((1, D)),
            _const_spec(wu.shape),
            _const_spec(cw.shape),
            _const_spec(cb.shape),
            _const_spec(wd.shape),
        ],
        out_specs=[
            pl.BlockSpec((1, tq, D), lambda b, t: (b, t, 0)),
            pl.BlockSpec((1, 2 * FFN_NCH, ns, FFN_CK), lambda b, t: (b, 0, 0, 0)),
        ],
        out_shape=[jax.ShapeDtypeStruct((B, T, D), F32),
                   jax.ShapeDtypeStruct((B, 2 * FFN_NCH, ns, FFN_CK), F32)],
        scratch_shapes=[pltpu.VMEM((tq, D), BF16),
                        pltpu.VMEM((2, halo + tq, FFN_CK), F32),
                        pltpu.VMEM((2 * FFN_NCH, halo, FFN_CK), F32)],
        compiler_params=_cparams("arbitrary", "arbitrary"),
        name="conv_ffn",
    )(x, prev, g, wu, cw, cb, wd)


def _oproj_kernel(x_ref, o_ref, w_ref, y_ref):
    y_ref[0] = x_ref[0] + _dot(o_ref[0], w_ref[...])


def _oproj_layer(x, o_bf, w_bf):
    B, T, D = x.shape
    tq = min(T, ROW_TILE)
    K = o_bf.shape[-1]
    return pl.pallas_call(
        _oproj_kernel,
        grid=(B, T // tq),
        in_specs=[
            pl.BlockSpec((1, tq, D), lambda b, t: (b, t, 0)),
            pl.BlockSpec((1, tq, K), lambda b, t: (b, t, 0)),
            _const_spec(w_bf.shape),
        ],
        out_specs=pl.BlockSpec((1, tq, D), lambda b, t: (b, t, 0)),
        out_shape=jax.ShapeDtypeStruct((B, T, D), F32),
        compiler_params=_cparams("arbitrary", "arbitrary"),
        name="out_proj",
    )(x, o_bf, w_bf)


def _seg_norm(a, seg_ref, segt_ref, gain):
    ms = _dot((a * a).astype(BF16), seg_ref[...]) * (1.0 / SEG)
    r = lax.rsqrt(ms + NORM_EPS)
    hi = r.astype(BF16)
    lo = (r - hi.astype(F32)).astype(BF16)
    rf = _dot(hi, segt_ref[...]) + _dot(lo, segt_ref[...])
    return a * rf * gain


def _rope_slab(s, cos, sin_lo, sin_hi):
    return s * cos + pltpu.roll(s, LANES - ROPE_DIM // 2, 1) * sin_lo + pltpu.roll(s, ROPE_DIM // 2, 1) * sin_hi


def _diff_proj_kernel(x_ref, g_ref, w_ref, seg_ref, segt_ref, qg_ref, kg_ref, cos_ref, s1_ref, s2_ref,
                      qb_ref, kf_ref, kb_ref, vf_ref, vb_ref, *, scale):
    hb = _rms_rows(x_ref[0], g_ref[...]).astype(BF16)
    cos, s1, s2 = cos_ref[...], s1_ref[...], s2_ref[...]
    q = _seg_norm(_dot(hb, w_ref[:, 0:D_MODEL]), seg_ref, segt_ref, qg_ref[...])
    for c in range(D_MODEL // LANES):
        sl = slice(c * LANES, (c + 1) * LANES)
        qb_ref[0, :, sl] = (_rope_slab(q[:, sl], cos, s1, s2) * scale).astype(BF16)
    k = _seg_norm(_dot(hb, w_ref[:, D_MODEL:2 * D_MODEL]), seg_ref, segt_ref, kg_ref[...])
    for c in range(D_MODEL // LANES):
        sl = slice(c * LANES, (c + 1) * LANES)
        kr = _rope_slab(k[:, sl], cos, s1, s2)
        kf_ref[0, :, sl] = kr
        kb_ref[0, :, sl] = kr.astype(BF16)
    v = _dot(hb, w_ref[:, 2 * D_MODEL:3 * D_MODEL])
    vf_ref[0] = v
    vb_ref[0] = v.astype(BF16)


def _diff_proj_layer(x, g, w_bf, seg, segt, qg, kg, cos, s1, s2):
    B, T, D = x.shape
    tq = min(T, ROW_TILE)
    row = pl.BlockSpec((1, tq, D), lambda b, t: (b, t, 0))
    tab = pl.BlockSpec((tq, LANES), lambda b, t: (t, 0))
    kern = functools.partial(_diff_proj_kernel, scale=DIFF_HD ** -0.5)
    return pl.pallas_call(
        kern,
        grid=(B, T // tq),
        in_specs=[row, _const_spec((1, D)), _const_spec(w_bf.shape), _const_spec(seg.shape),
                  _const_spec(segt.shape), _const_spec((1, D)), _const_spec((1, D)), tab, tab, tab],
        out_specs=[row, row, row, row, row],
        out_shape=[jax.ShapeDtypeStruct((B, T, D), BF16), jax.ShapeDtypeStruct((B, T, D), F32),
                   jax.ShapeDtypeStruct((B, T, D), BF16), jax.ShapeDtypeStruct((B, T, D), F32),
                   jax.ShapeDtypeStruct((B, T, D), BF16)],
        compiler_params=_cparams("arbitrary", "arbitrary"),
        name="diff_qkv_proj",
    )(x, g, w_bf, seg, segt, qg, kg, cos, s1, s2)


def _fox_proj_kernel(x_ref, g_ref, w_ref, wf_ref, bf_ref, seg_ref, segt_ref, qg_ref, kg_ref,
                     qb_ref, kf_ref, kb_ref, vf_ref, vb_ref, lf_ref, cum_ref, run,
                     *, tq, shift, scale):
    t = pl.program_id(1)

    @pl.when(t == 0)
    def _():
        run[...] = jnp.zeros_like(run)

    hb = _rms_rows(x_ref[0], g_ref[...]).astype(BF16)
    q = _seg_norm(_dot(hb, w_ref[:, 0:D_MODEL]), seg_ref, segt_ref, qg_ref[...])
    qb_ref[0] = (q * scale).astype(BF16)
    k = _seg_norm(_dot(hb, w_ref[:, D_MODEL:2 * D_MODEL]), seg_ref, segt_ref, kg_ref[...])
    kf_ref[0] = k
    kb_ref[0] = k.astype(BF16)
    v = _dot(hb, w_ref[:, 2 * D_MODEL:3 * D_MODEL])
    vf_ref[0] = v
    vb_ref[0] = v.astype(BF16)
    z = _dot(hb, wf_ref[...]) + bf_ref[...]
    lf = jnp.minimum(z, 0.0) - jnp.log1p(jnp.exp(-jnp.abs(z)))
    lf_ref[0] = lf
    m = jnp.where(_time_mask(tq, shift), 1.0, 0.0).astype(BF16)
    hi, mid, lo = _split3(lf)
    cum = _dot(m, hi) + _dot(m, mid) + _dot(m, lo) + run[...]
    cum_ref[0] = cum
    run[...] = cum[tq - 1:tq, :]


def _fox_proj_layer(x, g, w_bf, wf_bf, bf, seg, segt, qg, kg, *, shift):
    B, T, D = x.shape
    tq = min(T, ROW_TILE)
    row = pl.BlockSpec((1, tq, D), lambda b, t: (b, t, 0))
    nar = pl.BlockSpec((1, tq, LANES), lambda b, t: (b, t, 0))
    kern = functools.partial(_fox_proj_kernel, tq=tq, shift=shift, scale=FOX_HD ** -0.5)
    return pl.pallas_call(
        kern,
        grid=(B, T // tq),
        in_specs=[row, _const_spec((1, D)), _const_spec(w_bf.shape), _const_spec(wf_bf.shape),
                  _const_spec((1, LANES)), _const_spec(seg.shape), _const_spec(segt.shape),
                  _const_spec((1, D)), _const_spec((1, D))],
        out_specs=[row, row, row, row, row, nar, nar],
        out_shape=[jax.ShapeDtypeStruct((B, T, D), BF16), jax.ShapeDtypeStruct((B, T, D), F32),
                   jax.ShapeDtypeStruct((B, T, D), BF16), jax.ShapeDtypeStruct((B, T, D), F32),
                   jax.ShapeDtypeStruct((B, T, D), BF16), jax.ShapeDtypeStruct((B, T, LANES), F32),
                   jax.ShapeDtypeStruct((B, T, LANES), F32)],
        scratch_shapes=[pltpu.VMEM((1, LANES), F32)],
        compiler_params=_cparams("arbitrary", "arbitrary"),
        name="fox_qkvf_proj",
    )(x, g, w_bf, wf_bf, bf, seg, segt, qg, kg)


def _softmax_step(s, v, m, l, acc):
    m_new = jnp.maximum(m, jnp.max(s, axis=-1, keepdims=True))
    alpha = jnp.exp(m - m_new)
    p = jnp.exp(s - m_new)
    l = alpha * l + jnp.sum(p, axis=-1, keepdims=True)
    acc = alpha * acc + _dot(p.astype(BF16), v)
    return m_new, l, acc


def _diff_lambda(lq1, lk1, lq2, lk2, lam_init):
    a = jnp.sum(lq1[...] * lk1[...], axis=-1, keepdims=True)
    b = jnp.sum(lq2[...] * lk2[...], axis=-1, keepdims=True)
    return jnp.exp(a) - jnp.exp(b) + lam_init


def _diff_attn_kernel(q_ref, k_ref, v_ref, lq1, lk1, lq2, lk2, on_ref, o_ref, *, tq, lam_init):
    qi = pl.program_id(2)
    q = q_ref[0]
    lo = lax.broadcasted_iota(jnp.int32, (1, LANES), 1) < DIFF_HD
    zero = jnp.zeros_like(q)
    qs = (jnp.where(lo, q, zero), jnp.where(lo, zero, q))

    def tile(kj, carry, masked):
        off = pl.multiple_of(kj * tq, tq)
        k = k_ref[0, pl.ds(off, tq), :]
        v = v_ref[0, pl.ds(off, tq), :]
        out = []
        for i in range(2):
            s = _dot_nt(qs[i], k)
            if masked:
                r = lax.broadcasted_iota(jnp.int32, (tq, tq), 0)
                c = lax.broadcasted_iota(jnp.int32, (tq, tq), 1)
                s = jnp.where(r >= c, s, -jnp.inf)
            out.extend(_softmax_step(s, v, *carry[3 * i:3 * i + 3]))
        return tuple(out)

    init = (jnp.full((tq, 1), -jnp.inf, F32), jnp.zeros((tq, 1), F32), jnp.zeros((tq, LANES), F32)) * 2
    carry = lax.fori_loop(0, qi, lambda kj, cr: tile(kj, cr, False), init)
    m1, l1, a1, m2, l2, a2 = tile(qi, carry, True)
    lam = _diff_lambda(lq1, lk1, lq2, lk2, lam_init)
    o = a1 / l1 - lam * (a2 / l2)
    o_ref[0] = (_rms_rows(o, on_ref[...]) * (1.0 - lam_init)).astype(BF16)


def _diff_attn_prompt(qb, kb, vb, lq1, lk1, lq2, lk2, onorm, lam_init):
    B, T, D = qb.shape
    tq = ATT_TILE
    kern = functools.partial(_diff_attn_kernel, tq=tq, lam_init=lam_init)
    vec = _const_spec((1, DIFF_HD))
    return pl.pallas_call(
        kern,
        grid=(B, DIFF_HEADS, T // tq),
        in_specs=[
            pl.BlockSpec((1, tq, LANES), lambda b, h, i: (b, i, h)),
            pl.BlockSpec((1, T, LANES), lambda b, h, i: (b, 0, h)),
            pl.BlockSpec((1, T, LANES), lambda b, h, i: (b, 0, h)),
            vec, vec, vec, vec, _const_spec((1, LANES)),
        ],
        out_specs=pl.BlockSpec((1, tq, LANES), lambda b, h, i: (b, i, h)),
        out_shape=jax.ShapeDtypeStruct((B, T, D), BF16),
        compiler_params=_cparams("arbitrary", "arbitrary", "arbitrary"),
        name="diff_attn_prompt",
    )(qb, kb, vb, lq1, lk1, lq2, lk2, onorm)


def _fox_attn_kernel(q_ref, k_ref, v_ref, cq_ref, ck_ref, o_ref, *, tq):
    qi = pl.program_id(2)
    q = q_ref[0]
    lo = lax.broadcasted_iota(jnp.int32, (1, LANES), 1) < FOX_HD
    zero = jnp.zeros_like(q)
    qs = (jnp.where(lo, q, zero), jnp.where(lo, zero, q))
    cq = cq_ref[0, 0]

    def tile(kj, carry, masked):
        off = pl.multiple_of(kj * tq, tq)
        k = k_ref[0, pl.ds(off, tq), :]
        v = v_ref[0, pl.ds(off, tq), :]
        ck = ck_ref[0, 0, kj]
        out = []
        for i in range(2):
            s = _dot_nt(qs[i], k) + (cq[:, i:i + 1] - ck[i:i + 1, :])
            if masked:
                r = lax.broadcasted_iota(jnp.int32, (tq, tq), 0)
                c = lax.broadcasted_iota(jnp.int32, (tq, tq), 1)
                s = jnp.where(r >= c, s, -jnp.inf)
            out.extend(_softmax_step(s, v, *carry[3 * i:3 * i + 3]))
        return tuple(out)

    init = (jnp.full((tq, 1), -jnp.inf, F32), jnp.zeros((tq, 1), F32), jnp.zeros((tq, LANES), F32)) * 2
    carry = lax.fori_loop(0, qi, lambda kj, cr: tile(kj, cr, False), init)
    m1, l1, a1, m2, l2, a2 = tile(qi, carry, True)
    o_ref[0] = jnp.where(lo, a1 / l1, a2 / l2).astype(BF16)


def _fox_attn_prompt(qb, kb, vb, cq, ck):
    B, T, D = qb.shape
    tq = ATT_TILE
    nh = FOX_HEADS // 2
    kern = functools.partial(_fox_attn_kernel, tq=tq)
    return pl.pallas_call(
        kern,
        grid=(B, nh, T // tq),
        in_specs=[
            pl.BlockSpec((1, tq, LANES), lambda b, h, i: (b, i, h)),
            pl.BlockSpec((1, T, LANES), lambda b, h, i: (b, 0, h)),
            pl.BlockSpec((1, T, LANES), lambda b, h, i: (b, 0, h)),
            pl.BlockSpec((1, 1, tq, 2), lambda b, h, i: (b, h, i, 0)),
            pl.BlockSpec((1, 1, T // tq, 2, tq), lambda b, h, i: (b, h, 0, 0, 0)),
        ],
        out_specs=pl.BlockSpec((1, tq, LANES), lambda b, h, i: (b, i, h)),
        out_shape=jax.ShapeDtypeStruct((B, T, D), BF16),
        compiler_params=_cparams("arbitrary", "arbitrary", "arbitrary"),
        name="fox_attn_prompt",
    )(qb, kb, vb, cq, ck)


def _page_specs(n_pages_step, width):
    P = n_pages_step

    def mk(p):
        return pl.BlockSpec((1, PAGE_SIZE, width), lambda b, j, pt: (pt[b, j * P + p], 0, 0))

    return [mk(p) for p in range(P)]


def _paged_softmax_update(qt, kpages, vpages, bias, m_sc, l_sc, acc_sc):
    k = jnp.concatenate([r[0].astype(BF16) for r in kpages], axis=0)
    v = jnp.concatenate([r[0].astype(BF16) for r in vpages], axis=0)
    s = _dot_nt(qt, k)
    if bias is not None:
        s = s + bias
    m, l, acc = _softmax_step(s, v, m_sc[...], l_sc[...], acc_sc[...])
    m_sc[...] = m
    l_sc[...] = l
    acc_sc[...] = acc


def _new_keys_update(qt, kn, vn, bias, allowed, m_sc, l_sc, acc_sc):
    s = _dot_nt(qt, kn)
    if bias is not None:
        s = s + bias
    s = jnp.where(allowed, s, -jnp.inf)
    m, l, acc = _softmax_step(s, vn, m_sc[...], l_sc[...], acc_sc[...])
    return acc / l


def _diff_sample_kernel(pt_ref, q_ref, *rest, P, nj, dec_seq, lam_init):
    kpages = rest[:P]
    vpages = rest[P:2 * P]
    kn_ref, vn_ref, lq1, lk1, lq2, lk2, on_ref, o_ref, qt_sc, m_sc, l_sc, acc_sc = rest[2 * P:]
    j = pl.program_id(1)
    nrow = 2 * dec_seq * DIFF_HEADS

    @pl.when(j == 0)
    def _():
        r = lax.broadcasted_iota(jnp.int32, (nrow, D_MODEL), 0)
        c = lax.broadcasted_iota(jnp.int32, (nrow, D_MODEL), 1)
        own = jnp.logical_and(c // LANES == r % DIFF_HEADS,
                              (c % LANES) // DIFF_HD == r // (dec_seq * DIFF_HEADS))
        q = q_ref[0]
        qt_sc[...] = jnp.where(own, q, jnp.zeros_like(q))
        m_sc[...] = jnp.full_like(m_sc, -jnp.inf)
        l_sc[...] = jnp.zeros_like(l_sc)
        acc_sc[...] = jnp.zeros_like(acc_sc)

    _paged_softmax_update(qt_sc[...], kpages, vpages, None, m_sc, l_sc, acc_sc)

    @pl.when(j == nj - 1)
    def _():
        npad = kn_ref.shape[1]
        r = lax.broadcasted_iota(jnp.int32, (nrow, npad), 0)
        c = lax.broadcasted_iota(jnp.int32, (nrow, npad), 1)
        allowed = c <= (r % (dec_seq * DIFF_HEADS)) // DIFF_HEADS
        o = _new_keys_update(qt_sc[...], kn_ref[0], vn_ref[0], None, allowed, m_sc, l_sc, acc_sc)
        half = nrow // 2
        lam = _diff_lambda(lq1, lk1, lq2, lk2, lam_init)
        od = o[0:half] - lam * o[half:nrow]
        rr = lax.broadcasted_iota(jnp.int32, (half, D_MODEL), 0)
        cc = lax.broadcasted_iota(jnp.int32, (half, D_MODEL), 1)
        od = jnp.where(cc // LANES == rr % DIFF_HEADS, od, 0.0)
        rows = [jnp.sum(od[t * DIFF_HEADS:(t + 1) * DIFF_HEADS], axis=0, keepdims=True)
                for t in range(dec_seq)]
        ot = jnp.concatenate(rows, axis=0)
        for h in range(DIFF_HEADS):
            sl = slice(h * LANES, (h + 1) * LANES)
            o_ref[0, :, sl] = (_rms_rows(ot[:, sl], on_ref[...]) * (1.0 - lam_init)).astype(BF16)


def _diff_attn_sample(page_table, q_rep, cache_k, cache_v, kn, vn, lq1, lk1, lq2, lk2, onorm, lam_init,
                      dec_seq):
    nb, npages = page_table.shape
    P = PAGES_PER_STEP
    nj = npages // P
    nrow = q_rep.shape[1]
    npad = kn.shape[1]
    kern = functools.partial(_diff_sample_kernel, P=P, nj=nj, dec_seq=dec_seq, lam_init=lam_init)
    per_seq = lambda shape: pl.BlockSpec(shape, lambda b, j, pt: (b, 0, 0))
    vec = pl.BlockSpec((1, DIFF_HD), lambda b, j, pt: (0, 0))
    grid_spec = pltpu.PrefetchScalarGridSpec(
        num_scalar_prefetch=1,
        grid=(nb, nj),
        in_specs=([per_seq((1, nrow, D_MODEL))] + _page_specs(P, D_MODEL) + _page_specs(P, D_MODEL)
                  + [per_seq((1, npad, D_MODEL)), per_seq((1, npad, D_MODEL)), vec, vec, vec, vec,
                     pl.BlockSpec((1, LANES), lambda b, j, pt: (0, 0))]),
        out_specs=per_seq((1, dec_seq, D_MODEL)),
        scratch_shapes=[pltpu.VMEM((nrow, D_MODEL), BF16), pltpu.VMEM((nrow, 1), F32),
                        pltpu.VMEM((nrow, 1), F32), pltpu.VMEM((nrow, D_MODEL), F32)],
    )
    return pl.pallas_call(
        kern,
        grid_spec=grid_spec,
        out_shape=jax.ShapeDtypeStruct((nb, dec_seq, D_MODEL), BF16),
        compiler_params=_cparams("arbitrary", "arbitrary"),
        name="diff_attn_sample",
    )(page_table, q_rep, *([cache_k] * P), *([cache_v] * P), kn, vn, lq1, lk1, lq2, lk2, onorm)


def _fox_cum_kernel(pt_ref, *rest, P):
    pages = rest[:P]
    out_ref, run = rest[P:]
    j = pl.program_id(1)

    @pl.when(j == 0)
    def _():
        run[...] = jnp.zeros_like(run)

    lf = jnp.concatenate([r[0] for r in pages], axis=0)
    n = lf.shape[0]
    r = lax.broadcasted_iota(jnp.int32, (n, n), 0)
    c = lax.broadcasted_iota(jnp.int32, (n, n), 1)
    u = jnp.where(r <= c, 1.0, 0.0).astype(BF16)
    hi, mid, lo = _split3(lf)
    cum = _dot_tn(hi, u) + _dot_tn(mid, u) + _dot_tn(lo, u) + run[...]
    out_ref[0] = cum
    run[...] = cum[:, n - 1:n]


def _fox_cum_sample(page_table, cache_logf):
    nb, npages = page_table.shape
    P = PAGES_PER_STEP
    nh = cache_logf.shape[-1]
    tk = P * PAGE_SIZE
    grid_spec = pltpu.PrefetchScalarGridSpec(
        num_scalar_prefetch=1,
        grid=(nb, npages // P),
        in_specs=_page_specs(P, nh),
        out_specs=pl.BlockSpec((1, nh, tk), lambda b, j, pt: (b, 0, j)),
        scratch_shapes=[pltpu.VMEM((nh, 1), F32)],
    )
    return pl.pallas_call(
        functools.partial(_fox_cum_kernel, P=P),
        grid_spec=grid_spec,
        out_shape=jax.ShapeDtypeStruct((nb, nh, npages * PAGE_SIZE), F32),
        compiler_params=_cparams("arbitrary", "arbitrary"),
        name="fox_past_cumsum",
    )(page_table, *([cache_logf] * P))


def _fox_sample_kernel(pt_ref, q_ref, *rest, P, nj, dec_seq):
    kpages = rest[:P]
    vpages = rest[P:2 * P]
    (cp_ref, kn_ref, vn_ref, ctot_ref, cnew_ref, ctott_ref, cnewt_ref, o_ref,
     qt_sc, m_sc, l_sc, acc_sc) = rest[2 * P:]
    j = pl.program_id(1)
    nrow = dec_seq * FOX_HEADS

    @pl.when(j == 0)
    def _():
        r = lax.broadcasted_iota(jnp.int32, (nrow, D_MODEL), 0)
        c = lax.broadcasted_iota(jnp.int32, (nrow, D_MODEL), 1)
        q = q_ref[0]
        qt_sc[...] = jnp.where(c // FOX_HD == r % FOX_HEADS, q, jnp.zeros_like(q))
        m_sc[...] = jnp.full_like(m_sc, -jnp.inf)
        l_sc[...] = jnp.zeros_like(l_sc)
        acc_sc[...] = jnp.zeros_like(acc_sc)

    cq = ctot_ref[0] + cnew_ref[0]
    ck = cp_ref[0]
    bias = cq - jnp.concatenate([ck] * dec_seq, axis=0)
    _paged_softmax_update(qt_sc[...], kpages, vpages, bias, m_sc, l_sc, acc_sc)

    @pl.when(j == nj - 1)
    def _():
        npad = kn_ref.shape[1]
        r = lax.broadcasted_iota(jnp.int32, (nrow, npad), 0)
        c = lax.broadcasted_iota(jnp.int32, (nrow, npad), 1)
        allowed = c <= r // FOX_HEADS
        ckn = ctott_ref[0] + cnewt_ref[0]
        bias_n = cq - jnp.concatenate([ckn] * dec_seq, axis=0)
        o = _new_keys_update(qt_sc[...], kn_ref[0], vn_ref[0], bias_n, allowed, m_sc, l_sc, acc_sc)
        rr = lax.broadcasted_iota(jnp.int32, (nrow, D_MODEL), 0)
        cc = lax.broadcasted_iota(jnp.int32, (nrow, D_MODEL), 1)
        o = jnp.where(cc // FOX_HD == rr % FOX_HEADS, o, 0.0)
        rows = [jnp.sum(o[t * FOX_HEADS:(t + 1) * FOX_HEADS], axis=0, keepdims=True)
                for t in range(dec_seq)]
        o_ref[0] = jnp.concatenate(rows, axis=0).astype(BF16)


def _fox_attn_sample(page_table, q_rep, cache_k, cache_v, cpast, kn, vn, ctot, cnew, ctott, cnewt, dec_seq):
    nb, npages = page_table.shape
    P = PAGES_PER_STEP
    nj = npages // P
    nrow = q_rep.shape[1]
    npad = kn.shape[1]
    tk = P * PAGE_SIZE
    kern = functools.partial(_fox_sample_kernel, P=P, nj=nj, dec_seq=dec_seq)
    per_seq = lambda shape: pl.BlockSpec(shape, lambda b, j, pt: (b, 0, 0))
    grid_spec = pltpu.PrefetchScalarGridSpec(
        num_scalar_prefetch=1,
        grid=(nb, nj),
        in_specs=([per_seq((1, nrow, D_MODEL))] + _page_specs(P, D_MODEL) + _page_specs(P, D_MODEL)
                  + [pl.BlockSpec((1, FOX_HEADS, tk), lambda b, j, pt: (b, 0, j)),
                     per_seq((1, npad, D_MODEL)), per_seq((1, npad, D_MODEL)),
                     per_seq((1, nrow, 1)), per_seq((1, nrow, 1)),
                     per_seq((1, FOX_HEADS, 1)), per_seq((1, FOX_HEADS, npad))]),
        out_specs=per_seq((1, dec_seq, D_MODEL)),
        scratch_shapes=[pltpu.VMEM((nrow, D_MODEL), BF16), pltpu.VMEM((nrow, 1), F32),
                        pltpu.VMEM((nrow, 1), F32), pltpu.VMEM((nrow, D_MODEL), F32)],
    )
    return pl.pallas_call(
        kern,
        grid_spec=grid_spec,
        out_shape=jax.ShapeDtypeStruct((nb, dec_seq, D_MODEL), BF16),
        compiler_params=_cparams("arbitrary", "arbitrary"),
        name="fox_attn_sample",
    )(page_table, q_rep, *([cache_k] * P), *([cache_v] * P), cpast, kn, vn, ctot, cnew, ctott, cnewt)


def _sgu_kernel(x_ref, g_ref, win_ref, vn_ref, wmix_ref, bias_ref, wout_ref, *rest,
                tq, shift, emit_v):
    if emit_v:
        y_ref, v_ref, um = rest
    else:
        y_ref, um = rest
    x = x_ref[0]
    hb = _rms_rows(x, g_ref[...]).astype(BF16)

    def gelu(a):
        return 0.5 * a * (1.0 + lax.erf(a * (1.0 / math.sqrt(2.0))))

    v = _rms_rows(gelu(_dot(hb, win_ref[:, SG_WIDTH:2 * SG_WIDTH])), vn_ref[...])
    if emit_v:
        v_ref[0] = v
    vb = v.astype(BF16)
    u = gelu(_dot(hb, win_ref[:, 0:SG_WIDTH]))
    mask = _time_mask(SG_CHUNK, shift)
    for g in range(SG_GROUPS):
        w = jnp.where(mask, wmix_ref[g], 0.0).astype(BF16)
        cs = slice(g * SG_GD, (g + 1) * SG_GD)
        for ch in range(tq // SG_CHUNK):
            rs = slice(ch * SG_CHUNK, (ch + 1) * SG_CHUNK)
            mixed = _dot(w, vb[rs, cs]) + bias_ref[:, g:g + 1]
            um[rs, cs] = (u[rs, cs] * mixed).astype(BF16)
    y_ref[0] = x + _dot(um[...], wout_ref[...])


def _sgu_layer(x, g, win_bf, vnorm, wmix, bias, wout_bf, *, shift, emit_v):
    B, T, D = x.shape
    tq = min(T, SGU_TILE)
    row = pl.BlockSpec((1, tq, D), lambda b, t: (b, t, 0))
    out_specs = [row]
    out_shape = [jax.ShapeDtypeStruct((B, T, D), F32)]
    if emit_v:
        out_specs.append(pl.BlockSpec((1, tq, SG_WIDTH), lambda b, t: (b, t, 0)))
        out_shape.append(jax.ShapeDtypeStruct((B, T, SG_WIDTH), F32))
    kern = functools.partial(_sgu_kernel, tq=tq, shift=shift, emit_v=emit_v)
    return pl.pallas_call(
        kern,
        grid=(B, T // tq),
        in_specs=[row, _const_spec((1, D)), _const_spec(win_bf.shape), _const_spec((1, SG_WIDTH)),
                  _const_spec(wmix.shape), _const_spec(bias.shape), _const_spec(wout_bf.shape)],
        out_specs=out_specs,
        out_shape=out_shape,
        scratch_shapes=[pltpu.VMEM((tq, SG_WIDTH), BF16)],
        compiler_params=_cparams("arbitrary", "arbitrary"),
        name="sgu_mixer",
    )(x, g, win_bf, vnorm, wmix, bias, wout_bf)


def _lambda_init(layer_idx):
    return 0.8 - 0.6 * math.exp(-0.3 * layer_idx)


def _rope_tables(pos):
    half = ROPE_DIM // 2
    inv = ROPE_THETA ** (-jnp.arange(half, dtype=F32) / half)
    ang = pos.astype(F32)[:, None] * inv[None, :]
    lane = jnp.arange(LANES) % SEG
    idx = lane % half
    cos = jnp.where(lane[None, :] < ROPE_DIM, jnp.cos(ang)[:, idx], 1.0)
    sin = jnp.sin(ang)[:, idx]
    s1 = jnp.where(lane[None, :] < half, -sin, 0.0)
    s2 = jnp.where(jnp.logical_and(lane[None, :] >= half, lane[None, :] < ROPE_DIM), sin, 0.0)
    return cos.astype(F32), s1.astype(F32), s2.astype(F32)


def _to_time_major(a):
    a = jnp.swapaxes(a, 0, 1)
    return a.reshape(1, a.shape[0] * a.shape[1], *a.shape[2:])


def _from_time_major(a, nb):
    a = a.reshape(a.shape[1] // nb, nb, *a.shape[2:])
    return jnp.swapaxes(a, 0, 1)


def kernel(x_prompt, x_sample, state_pool, cache_diff_k, cache_diff_v, cache_fox_k, cache_fox_v,
           cache_fox_logf, state_ffn_conv, page_table, norm_mix, norm_ffn, pool_w, pool_scale,
           diff_w_qkv, diff_q_norm, diff_k_norm, diff_lq1, diff_lk1, diff_lq2, diff_lk2, diff_o_norm,
           diff_w_o, fox_w_qkvf, fox_b_f, fox_q_norm, fox_k_norm, fox_w_o, sgu_w_in, sgu_v_norm,
           sgu_w_s, sgu_b_s, sgu_w_out, ffn_w_up, ffn_conv_w, ffn_conv_b, ffn_w_down):
    bp, seq, _ = x_prompt.shape
    nb, dec_seq, _ = x_sample.shape
    n_pool = cache_diff_k.shape[0]
    past_len = page_table.shape[1] * PAGE_SIZE
    row2 = lambda a: a.reshape(1, -1).astype(F32)

    pool_w_bf = pool_w.astype(BF16)
    wu = ffn_w_up.astype(BF16).reshape(DEPTH, D_MODEL, 2 * FFN_NCH, FFN_CK).transpose(0, 2, 1, 3)
    wd = ffn_w_down.astype(BF16).reshape(DEPTH, FFN_NCH, FFN_CK, D_MODEL)
    cw = ffn_conv_w.reshape(DEPTH, FFN_CONV, 2 * FFN_NCH, FFN_CK).transpose(0, 2, 1, 3)
    cb = ffn_conv_b.reshape(DEPTH, 2 * FFN_NCH, 1, FFN_CK)
    seg_id = jnp.arange(D_MODEL) // SEG
    seg = (seg_id[:, None] == jnp.arange(LANES)[None, :]).astype(BF16)
    segt = seg.T
    diff_w_bf = diff_w_qkv.astype(BF16)
    diff_wo_bf = diff_w_o.astype(BF16)
    fox_w_bf = fox_w_qkvf[:, :3 * D_MODEL].astype(BF16)
    fox_wf_bf = jnp.pad(fox_w_qkvf[:, 3 * D_MODEL:], ((0, 0), (0, LANES - FOX_HEADS))).astype(BF16)
    fox_bf = jnp.pad(fox_b_f, (0, LANES - FOX_HEADS)).reshape(1, LANES)
    fox_wo_bf = fox_w_o.astype(BF16)
    tile_seg = lambda g: jnp.tile(g, NSEG).reshape(1, D_MODEL)
    sgu_win_bf = sgu_w_in.astype(BF16)
    sgu_wout_bf = sgu_w_out.astype(BF16)
    lvec = [row2(a) for a in (diff_lq1, diff_lk1, diff_lq2, diff_lk2)]
    onorm = row2(diff_o_norm)

    def trunk(x, *, shift, start, pool_prev, conv_prev, rope, sgu_mix, sgu_bias, attn_diff, attn_fox, emit_v):
        new = {}
        conv_new = []
        for i in range(DEPTH):
            g = row2(norm_mix[i])
            if i == 0:
                x, new["pool"] = _pool_layer(x, pool_prev, g, pool_w_bf, row2(pool_scale),
                                             shift=shift, start=start)
            elif i == 1:
                qb, kf, kb, vf, vb = _diff_proj_layer(x, g, diff_w_bf, seg, segt, tile_seg(diff_q_norm),
                                                      tile_seg(diff_k_norm), *rope)
                new["diff_k"], new["diff_v"] = kf, vf
                x = _oproj_layer(x, attn_diff(qb, kb, vb, _lambda_init(i)), diff_wo_bf)
            elif i == 2:
                qb, kf, kb, vf, vb, lf, cum = _fox_proj_layer(
                    x, g, fox_w_bf, fox_wf_bf, fox_bf, seg, segt, tile_seg(fox_q_norm),
                    tile_seg(fox_k_norm), shift=shift)
                new["fox_k"], new["fox_v"], new["fox_logf"] = kf, vf, lf[..., :FOX_HEADS]
                x = _oproj_layer(x, attn_fox(qb, kb, vb, cum[..., :FOX_HEADS]), fox_wo_bf)
            else:
                res = _sgu_layer(x, g, sgu_win_bf, row2(sgu_v_norm), sgu_mix, sgu_bias, sgu_wout_bf,
                                 shift=shift, emit_v=emit_v)
                x = res[0]
                if emit_v:
                    new["sgu_v"] = res[1]
            x, st = _ffn_layer(x, conv_prev[i], row2(norm_ffn[i]), wu[i], cw[i], cb[i], wd[i], shift=shift)
            conv_new.append(st)
        return x, new, conv_new

    halo_p = SUBLANES
    rope_p = _rope_tables(jnp.arange(seq, dtype=jnp.int32))

    def attn_fox_prompt(qb, kb, vb, cum):
        c4 = cum.reshape(bp, seq, FOX_HEADS // 2, 2)
        ck = c4.reshape(bp, seq // ATT_TILE, ATT_TILE, FOX_HEADS // 2, 2).transpose(0, 3, 1, 4, 2)
        return _fox_attn_prompt(qb, kb, vb, c4.transpose(0, 2, 1, 3), ck)

    yp, newp, convp = trunk(
        x_prompt, shift=1, start=0,
        pool_prev=jnp.zeros((bp, 2 * SUBLANES, D_MODEL), F32),
        conv_prev=[jnp.zeros((bp, 2 * FFN_NCH, halo_p, FFN_CK), F32)] * DEPTH,
        rope=rope_p,
        sgu_mix=sgu_w_s, sgu_bias=sgu_b_s.T,
        attn_diff=lambda qb, kb, vb, li: _diff_attn_prompt(qb, kb, vb, *lvec, onorm, li),
        attn_fox=attn_fox_prompt, emit_v=False)

    rows = dec_seq * nb
    t_of_row = jnp.arange(rows, dtype=jnp.int32) // nb
    rope_s = _rope_tables(past_len + t_of_row)
    ck_cache = cache_diff_k.reshape(n_pool, PAGE_SIZE, D_MODEL)
    cv_cache = cache_diff_v.reshape(n_pool, PAGE_SIZE, D_MODEL)
    fk_cache = cache_fox_k.reshape(n_pool, PAGE_SIZE, D_MODEL)
    fv_cache = cache_fox_v.reshape(n_pool, PAGE_SIZE, D_MODEL)
    npad = 2 * SUBLANES

    def seq_major(a):
        return _from_time_major(a, nb)

    def pad_new(a):
        return jnp.pad(seq_major(a), ((0, 0), (0, npad - dec_seq), (0, 0)))

    def attn_diff_sample(qb, kb, vb, li):
        q_rep = jnp.tile(jnp.repeat(seq_major(qb), DIFF_HEADS, axis=1), (1, 2, 1))
        o = _diff_attn_sample(page_table, q_rep, ck_cache, cv_cache, pad_new(kb), pad_new(vb),
                              *lvec, onorm, li, dec_seq)
        return _to_time_major(o)

    def attn_fox_sample(qb, kb, vb, cum):
        cpast = _fox_cum_sample(page_table, cache_fox_logf)
        ctot = cpast[:, :, -1]
        cnew = seq_major(cum)
        q_rep = jnp.repeat(seq_major(qb), FOX_HEADS, axis=1)
        ctot_col = jnp.tile(ctot, (1, dec_seq)).reshape(nb, dec_seq * FOX_HEADS, 1)
        cnew_col = cnew.reshape(nb, dec_seq * FOX_HEADS, 1)
        cnewt = jnp.pad(jnp.swapaxes(cnew, 1, 2), ((0, 0), (0, 0), (0, npad - dec_seq)))
        o = _fox_attn_sample(page_table, q_rep, fk_cache, fv_cache, cpast, pad_new(kb), pad_new(vb),
                             ctot_col, cnew_col, ctot[:, :, None], cnewt, dec_seq)
        return _to_time_major(o)

    halo_s = (FFN_CONV - 1) * nb
    conv_prev_s = [
        state_ffn_conv[i].reshape(nb, FFN_CONV - 1, 2 * FFN_NCH, FFN_CK).transpose(2, 1, 0, 3)
        .reshape(1, 2 * FFN_NCH, halo_s, FFN_CK) for i in range(DEPTH)]
    sgu_mix_s = jnp.repeat(jnp.repeat(sgu_w_s[:, :dec_seq, :dec_seq], nb, axis=1), nb, axis=2)
    sgu_bias_s = jnp.repeat(sgu_b_s[:, :dec_seq].T, nb, axis=0)
    ys, news, convs = trunk(
        _to_time_major(x_sample), shift=nb, start=past_len,
        pool_prev=_to_time_major(state_pool),
        conv_prev=conv_prev_s, rope=rope_s,
        sgu_mix=sgu_mix_s, sgu_bias=sgu_bias_s,
        attn_diff=attn_diff_sample, attn_fox=attn_fox_sample, emit_v=True)

    def conv_state_prompt(st):
        return st.transpose(0, 2, 1, 3).reshape(bp, FFN_CONV - 1, 2 * FFN_DIM)

    def conv_state_sample(st):
        return (st.reshape(2 * FFN_NCH, FFN_CONV - 1, nb, FFN_CK).transpose(2, 1, 0, 3)
                .reshape(nb, FFN_CONV - 1, 2 * FFN_DIM))

    hd4 = lambda a, h: a.reshape(a.shape[0], a.shape[1], h, D_MODEL // h)
    sm = seq_major
    return (
        yp, sm(ys),
        newp["pool"], sm(news["pool"]),
        hd4(newp["diff_k"], DIFF_HEADS), hd4(newp["diff_v"], DIFF_HEADS),
        hd4(sm(news["diff_k"]), DIFF_HEADS), hd4(sm(news["diff_v"]), DIFF_HEADS),
        hd4(newp["fox_k"], FOX_HEADS), hd4(newp["fox_v"], FOX_HEADS), newp["fox_logf"],
        hd4(sm(news["fox_k"]), FOX_HEADS), hd4(sm(news["fox_v"]), FOX_HEADS), sm(news["fox_logf"]),
        sm(news["sgu_v"]),
        jnp.stack([conv_state_prompt(s) for s in convp], axis=0),
        jnp.stack([conv_state_sample(s) for s in convs], axis=0),
    )
```

```python
import functools
import math

import jax
import jax.numpy as jnp
from jax import lax
from jax.experimental import pallas as pl
from jax.experimental.pallas import tpu as pltpu

F32 = jnp.float32
BF16 = jnp.bfloat16

D_MODEL = 1024
DEPTH = 4
PAGE_SIZE = 128
POOL_WINDOWS = (2, 4, 8, 16)
POOL_GD = D_MODEL // len(POOL_WINDOWS)
POOL_BUF = max(POOL_WINDOWS) - 1
DIFF_HEADS = 8
DIFF_HD = 64
FOX_HEADS = 16
FOX_HD = 64
SG_WIDTH = 2 * D_MODEL
SG_GROUPS = 8
SG_GD = SG_WIDTH // SG_GROUPS
SG_CHUNK = 128
FFN_DIM = 2816
FFN_CONV = 3
ROPE_THETA = 500000.0
ROPE_DIM = DIFF_HD // 4
NORM_EPS = 1e-6

LANES = 128
SUBLANES = 8
FFN_CK = 256
FFN_NCH = FFN_DIM // FFN_CK
SEG = 64
NSEG = D_MODEL // SEG
PAGES_PER_STEP = 8
ROW_TILE = 512
SGU_TILE = 256
ATT_TILE = 512
VMEM_LIMIT = 56 * 1024 * 1024
LOG2E = math.log2(math.e)


def _cparams(*sem):
    return pltpu.CompilerParams(dimension_semantics=sem, vmem_limit_bytes=VMEM_LIMIT)


def _rms_rows(x, g):
    ms = jnp.mean(x * x, axis=-1, keepdims=True)
    return x * lax.rsqrt(ms + NORM_EPS) * g


def _dot(a, b):
    return jnp.dot(a, b, preferred_element_type=F32)


def _dot_nt(a, b):
    return lax.dot_general(a, b, (((1,), (1,)), ((), ())), preferred_element_type=F32)


def _split3(a):
    hi = a.astype(BF16)
    r1 = a - hi.astype(F32)
    mid = r1.astype(BF16)
    lo = (r1 - mid.astype(F32)).astype(BF16)
    return hi, mid, lo


def _time_mask(n, shift):
    r = lax.broadcasted_iota(jnp.int32, (n, n), 0)
    c = lax.broadcasted_iota(jnp.int32, (n, n), 1)
    m = c <= r
    if shift > 1:
        m = jnp.logical_and(m, (r - c) % shift == 0)
    return m


def _const_spec(shape):
    nd = len(shape)
    return pl.BlockSpec(shape, lambda *_: (0,) * nd, pipeline_mode=pl.Buffered(1))


def _pool_kernel(x_ref, prev_ref, g_ref, w_ref, scale_ref, y_ref, st_ref, hext,
                 *, tq, shift, halo, start, nt):
    t = pl.program_id(1)

    @pl.when(t == 0)
    def _():
        hext[0:halo, :] = prev_ref[0]

    x = x_ref[0]
    h = _rms_rows(x, g_ref[...])
    hext[halo:halo + tq, :] = h
    row = lax.broadcasted_iota(jnp.int32, (tq, 1), 0)
    pos = start + (t * tq + row) // shift
    for g, win in enumerate(POOL_WINDOWS):
        c0 = g * POOL_GD
        hg = h[:, c0:c0 + POOL_GD]
        s = hg
        for j in range(1, win):
            o = halo - j * shift
            s = s + hext[o:o + tq, c0:c0 + POOL_GD]
        cnt = jnp.minimum(pos + 1, win).astype(F32)
        d = (s / cnt - hg).astype(BF16)
        y = _dot(d, w_ref[g])
        y_ref[0, :, c0:c0 + POOL_GD] = x[:, c0:c0 + POOL_GD] + y * scale_ref[:, c0:c0 + POOL_GD]

    n = POOL_BUF * shift

    @pl.when(t == nt - 1)
    def _():
        st_ref[0] = hext[halo + tq - n:halo + tq, :]

    if nt > 1:
        hext[0:halo, :] = hext[tq:tq + halo, :]


def _pool_layer(x, prev, g, w_bf, scale, *, shift, start):
    B, T, D = x.shape
    tq = min(T, ROW_TILE)
    nt = T // tq
    halo = prev.shape[1]
    n = POOL_BUF * shift
    kern = functools.partial(_pool_kernel, tq=tq, shift=shift, halo=halo, start=start, nt=nt)
    return pl.pallas_call(
        kern,
        grid=(B, nt),
        in_specs=[
            pl.BlockSpec((1, tq, D), lambda b, t: (b, t, 0)),
            pl.BlockSpec((1, halo, D), lambda b, t: (b, 0, 0)),
            _const_spec((1, D)),
            _const_spec(w_bf.shape),
            _const_spec((1, D)),
        ],
        out_specs=[
            pl.BlockSpec((1, tq, D), lambda b, t: (b, t, 0)),
            pl.BlockSpec((1, n, D), lambda b, t: (b, 0, 0)),
        ],
        out_shape=[jax.ShapeDtypeStruct((B, T, D), F32), jax.ShapeDtypeStruct((B, n, D), F32)],
        scratch_shapes=[pltpu.VMEM((halo + tq, D), F32)],
        compiler_params=_cparams("arbitrary", "arbitrary"),
        name="pool_mixer",
    )(x, prev, g, w_bf, scale)


def _ffn_kernel(x_ref, prev_ref, g_ref, wu_ref, cw_ref, cb_ref, wd_ref, y_ref, st_ref,
                hb, buf, act, *, tq, shift, halo, nt):
    t = pl.program_id(1)
    x = x_ref[0]
    hb[...] = _rms_rows(x, g_ref[...]).astype(BF16)

    @pl.when(t == 0)
    def _():
        buf[:, 0:halo, :] = prev_ref[0]

    def conv(idx):
        u = _dot(hb[...], wu_ref[idx])
        buf[idx, halo:halo + tq, :] = u
        w = cw_ref[idx]
        o2 = halo - 2 * shift
        o1 = halo - shift
        out = (cb_ref[idx] + w[0:1] * buf[idx, o2:o2 + tq, :]
               + w[1:2] * buf[idx, o1:o1 + tq, :] + w[2:3] * u)
        buf[idx, 0:halo, :] = buf[idx, tq:tq + halo, :]
        return out

    for c in range(FFN_NCH):
        ag = conv(c)
        av = conv(FFN_NCH + c)
        act[:, c * FFN_CK:(c + 1) * FFN_CK] = (ag / (1.0 + jnp.exp(-ag)) * av).astype(BF16)
    y_ref[0] = x + _dot(act[...], wd_ref[...])

    @pl.when(t == nt - 1)
    def _():
        st_ref[0] = buf[:, halo - 2 * shift:halo, :]


def _ffn_layer(x, prev, g, wu, cw, cb, wd, *, shift):
    B, T, D = x.shape
    tq = min(T, ROW_TILE)
    nt = T // tq
    halo = prev.shape[2]
    ns = (FFN_CONV - 1) * shift
    kern = functools.partial(_ffn_kernel, tq=tq, shift=shift, halo=halo, nt=nt)
    return pl.pallas_call(
        kern,
        grid=(B, nt),
        in_specs=[
            pl.BlockSpec((1, tq, D), lambda b, t: (b, t, 0)),
            pl.BlockSpec((1, 2 * FFN_NCH, halo, FFN_CK), lambda b, t: (b, 0, 0, 0)),
            _const_spec((1, D)),
            _const_spec(wu.shape),
            _const_spec(cw.shape),
            _const_spec(cb.shape),
            _const_spec(wd.shape),
        ],
        out_specs=[
            pl.BlockSpec((1, tq, D), lambda b, t: (b, t, 0)),
            pl.BlockSpec((1, 2 * FFN_NCH, ns, FFN_CK), lambda b, t: (b, 0, 0, 0)),
        ],
        out_shape=[jax.ShapeDtypeStruct((B, T, D), F32),
                   jax.ShapeDtypeStruct((B, 2 * FFN_NCH, ns, FFN_CK), F32)],
        scratch_shapes=[pltpu.VMEM((tq, D), BF16),
                        pltpu.VMEM((2 * FFN_NCH, halo + tq, FFN_CK), F32),
                        pltpu.VMEM((tq, FFN_DIM), BF16)],
        compiler_params=_cparams("arbitrary", "arbitrary"),
        name="conv_ffn",
    )(x, prev, g, wu, cw, cb, wd)


def _oproj_kernel(x_ref, o_ref, w_ref, y_ref):
    y_ref[0] = x_ref[0] + _dot(o_ref[0], w_ref[...])


def _oproj_layer(x, o_bf, w_bf):
    B, T, D = x.shape
    tq = min(T, ROW_TILE)
    K = o_bf.shape[-1]
    return pl.pallas_call(
        _oproj_kernel,
        grid=(B, T // tq),
        in_specs=[
            pl.BlockSpec((1, tq, D), lambda b, t: (b, t, 0)),
            pl.BlockSpec((1, tq, K), lambda b, t: (b, t, 0)),
            _const_spec(w_bf.shape),
        ],
        out_specs=pl.BlockSpec((1, tq, D), lambda b, t: (b, t, 0)),
        out_shape=jax.ShapeDtypeStruct((B, T, D), F32),
        compiler_params=_cparams("arbitrary", "arbitrary"),
        name="out_proj",
    )(x, o_bf, w_bf)


def _seg_norm(a, seg_ref, segt_ref, gain):
    ms = _dot((a * a).astype(BF16), seg_ref[...]) * (1.0 / SEG)
    r = lax.rsqrt(ms + NORM_EPS)
    hi = r.astype(BF16)
    lo = (r - hi.astype(F32)).astype(BF16)
    rf = _dot(hi, segt_ref[...]) + _dot(lo, segt_ref[...])
    return a * rf * gain


def _rope_slab(s, cos, sin_lo, sin_hi):
    return s * cos + pltpu.roll(s, LANES - ROPE_DIM // 2, 1) * sin_lo + pltpu.roll(s, ROPE_DIM // 2, 1) * sin_hi


def _diff_proj_kernel(x_ref, g_ref, w_ref, seg_ref, segt_ref, qg_ref, kg_ref, cos_ref, s1_ref, s2_ref,
                      qb_ref, kf_ref, kb_ref, vf_ref, vb_ref, *, scale):
    hb = _rms_rows(x_ref[0], g_ref[...]).astype(BF16)
    cos, s1, s2 = cos_ref[...], s1_ref[...], s2_ref[...]
    q = _seg_norm(_dot(hb, w_ref[:, 0:D_MODEL]), seg_ref, segt_ref, qg_ref[...])
    for c in range(D_MODEL // LANES):
        sl = slice(c * LANES, (c + 1) * LANES)
        qb_ref[0, :, sl] = (_rope_slab(q[:, sl], cos, s1, s2) * scale).astype(BF16)
    k = _seg_norm(_dot(hb, w_ref[:, D_MODEL:2 * D_MODEL]), seg_ref, segt_ref, kg_ref[...])
    for c in range(D_MODEL // LANES):
        sl = slice(c * LANES, (c + 1) * LANES)
        kr = _rope_slab(k[:, sl], cos, s1, s2)
        kf_ref[0, :, sl] = kr
        kb_ref[0, :, sl] = kr.astype(BF16)
    v = _dot(hb, w_ref[:, 2 * D_MODEL:3 * D_MODEL])
    vf_ref[0] = v
    vb_ref[0] = v.astype(BF16)


def _diff_proj_layer(x, g, w_bf, seg, segt, qg, kg, cos, s1, s2):
    B, T, D = x.shape
    tq = min(T, ROW_TILE)
    row = pl.BlockSpec((1, tq, D), lambda b, t: (b, t, 0))
    tab = pl.BlockSpec((tq, LANES), lambda b, t: (t, 0))
    kern = functools.partial(_diff_proj_kernel, scale=DIFF_HD ** -0.5 * LOG2E)
    return pl.pallas_call(
        kern,
        grid=(B, T // tq),
        in_specs=[row, _const_spec((1, D)), _const_spec(w_bf.shape), _const_spec(seg.shape),
                  _const_spec(segt.shape), _const_spec((1, D)), _const_spec((1, D)), tab, tab, tab],
        out_specs=[row, row, row, row, row],
        out_shape=[jax.ShapeDtypeStruct((B, T, D), BF16), jax.ShapeDtypeStruct((B, T, D), F32),
                   jax.ShapeDtypeStruct((B, T, D), BF16), jax.ShapeDtypeStruct((B, T, D), F32),
                   jax.ShapeDtypeStruct((B, T, D), BF16)],
        compiler_params=_cparams("arbitrary", "arbitrary"),
        name="diff_qkv_proj",
    )(x, g, w_bf, seg, segt, qg, kg, cos, s1, s2)


def _fox_proj_kernel(x_ref, g_ref, w_ref, wf_ref, bf_ref, seg_ref, segt_ref, qg_ref, kg_ref,
                     qb_ref, kf_ref, kb_ref, vf_ref, vb_ref, lf_ref, nck_ref, run,
                     *, tq, shift, scale):
    t = pl.program_id(1)

    @pl.when(t == 0)
    def _():
        run[...] = jnp.zeros_like(run)

    hb = _rms_rows(x_ref[0], g_ref[...]).astype(BF16)
    q = _seg_norm(_dot(hb, w_ref[:, 0:D_MODEL]), seg_ref, segt_ref, qg_ref[...])
    qb_ref[0] = (q * scale).astype(BF16)
    k = _seg_norm(_dot(hb, w_ref[:, D_MODEL:2 * D_MODEL]), seg_ref, segt_ref, kg_ref[...])
    kf_ref[0] = k
    kb_ref[0] = k.astype(BF16)
    v = _dot(hb, w_ref[:, 2 * D_MODEL:3 * D_MODEL])
    vf_ref[0] = v
    vb_ref[0] = v.astype(BF16)
    z = _dot(hb, wf_ref[...]) + bf_ref[...]
    lf = jnp.minimum(z, 0.0) - jnp.log1p(jnp.exp(-jnp.abs(z)))
    lf_ref[0] = lf
    m = jnp.where(_time_mask(tq, shift), 1.0, 0.0).astype(BF16)
    hi, mid, lo = _split3(lf)
    cum = _dot(m, hi) + _dot(m, mid) + _dot(m, lo) + run[...]
    nck_ref[0] = cum * (-LOG2E)
    run[...] = cum[tq - 1:tq, :]


def _fox_proj_layer(x, g, w_bf, wf_bf, bf, seg, segt, qg, kg, *, shift):
    B, T, D = x.shape
    tq = min(T, ROW_TILE)
    row = pl.BlockSpec((1, tq, D), lambda b, t: (b, t, 0))
    nar = pl.BlockSpec((1, tq, LANES), lambda b, t: (b, t, 0))
    kern = functools.partial(_fox_proj_kernel, tq=tq, shift=shift, scale=FOX_HD ** -0.5 * LOG2E)
    return pl.pallas_call(
        kern,
        grid=(B, T // tq),
        in_specs=[row, _const_spec((1, D)), _const_spec(w_bf.shape), _const_spec(wf_bf.shape),
                  _const_spec((1, LANES)), _const_spec(seg.shape), _const_spec(segt.shape),
                  _const_spec((1, D)), _const_spec((1, D))],
        out_specs=[row, row, row, row, row, nar, nar],
        out_shape=[jax.ShapeDtypeStruct((B, T, D), BF16), jax.ShapeDtypeStruct((B, T, D), F32),
                   jax.ShapeDtypeStruct((B, T, D), BF16), jax.ShapeDtypeStruct((B, T, D), F32),
                   jax.ShapeDtypeStruct((B, T, D), BF16), jax.ShapeDtypeStruct((B, T, LANES), F32),
                   jax.ShapeDtypeStruct((B, T, LANES), F32)],
        scratch_shapes=[pltpu.VMEM((1, LANES), F32)],
        compiler_params=_cparams("arbitrary", "arbitrary"),
        name="fox_qkvf_proj",
    )(x, g, w_bf, wf_bf, bf, seg, segt, qg, kg)


def _softmax_step(s, v, m, l, acc):
    m_new = jnp.maximum(m, jnp.max(s, axis=-1, keepdims=True))
    alpha = jnp.exp2(m - m_new)
    p = jnp.exp2(s - m_new)
    l = alpha * l + jnp.sum(p, axis=-1, keepdims=True)
    acc = alpha * acc + _dot(p.astype(BF16), v)
    return m_new, l, acc


def _diff_lambda(lq1, lk1, lq2, lk2, lam_init):
    a = jnp.sum(lq1[...] * lk1[...], axis=-1, keepdims=True)
    b = jnp.sum(lq2[...] * lk2[...], axis=-1, keepdims=True)
    return jnp.exp(a) - jnp.exp(b) + lam_init


def _diff_attn_kernel(q_ref, k_ref, v_ref, lq1, lk1, lq2, lk2, on_ref, o_ref, *, tq, lam_init):
    qi = pl.program_id(2)
    q = q_ref[0]
    lo = lax.broadcasted_iota(jnp.int32, (1, LANES), 1) < DIFF_HD
    zero = jnp.zeros_like(q)
    qs = (jnp.where(lo, q, zero), jnp.where(lo, zero, q))

    def tile(kj, carry, masked):
        off = pl.multiple_of(kj * tq, tq)
        k = k_ref[0, pl.ds(off, tq), :]
        v = v_ref[0, pl.ds(off, tq), :]
        out = []
        for i in range(2):
            s = _dot_nt(qs[i], k)
            if masked:
                r = lax.broadcasted_iota(jnp.int32, (tq, tq), 0)
                c = lax.broadcasted_iota(jnp.int32, (tq, tq), 1)
                s = jnp.where(r >= c, s, -jnp.inf)
            out.extend(_softmax_step(s, v, *carry[3 * i:3 * i + 3]))
        return tuple(out)

    init = (jnp.full((tq, 1), -jnp.inf, F32), jnp.zeros((tq, 1), F32), jnp.zeros((tq, LANES), F32)) * 2
    carry = lax.fori_loop(0, qi, lambda kj, cr: tile(kj, cr, False), init)
    m1, l1, a1, m2, l2, a2 = tile(qi, carry, True)
    lam = _diff_lambda(lq1, lk1, lq2, lk2, lam_init)
    o = a1 / l1 - lam * (a2 / l2)
    o_ref[0] = (_rms_rows(o, on_ref[...]) * (1.0 - lam_init)).astype(BF16)


def _diff_attn_prompt(qb, kb, vb, lq1, lk1, lq2, lk2, onorm, lam_init):
    B, T, D = qb.shape
    tq = ATT_TILE
    kern = functools.partial(_diff_attn_kernel, tq=tq, lam_init=lam_init)
    vec = _const_spec((1, DIFF_HD))
    return pl.pallas_call(
        kern,
        grid=(B, DIFF_HEADS, T // tq),
        in_specs=[
            pl.BlockSpec((1, tq, LANES), lambda b, h, i: (b, i, h)),
            pl.BlockSpec((1, T, LANES), lambda b, h, i: (b, 0, h)),
            pl.BlockSpec((1, T, LANES), lambda b, h, i: (b, 0, h)),
            vec, vec, vec, vec, _const_spec((1, LANES)),
        ],
        out_specs=pl.BlockSpec((1, tq, LANES), lambda b, h, i: (b, i, h)),
        out_shape=jax.ShapeDtypeStruct((B, T, D), BF16),
        compiler_params=_cparams("arbitrary", "arbitrary", "arbitrary"),
        name="diff_attn_prompt",
    )(qb, kb, vb, lq1, lk1, lq2, lk2, onorm)


def _fox_attn_kernel(q_ref, k_ref, v_ref, nck_ref, o_ref, *, tq):
    qi = pl.program_id(2)
    q = q_ref[0]
    lo = lax.broadcasted_iota(jnp.int32, (1, LANES), 1) < FOX_HD
    zero = jnp.zeros_like(q)
    qs = (jnp.where(lo, q, zero), jnp.where(lo, zero, q))

    def tile(kj, carry, masked):
        off = pl.multiple_of(kj * tq, tq)
        k = k_ref[0, pl.ds(off, tq), :]
        v = v_ref[0, pl.ds(off, tq), :]
        nck = nck_ref[0, 0, kj]
        out = []
        for i in range(2):
            s = _dot_nt(qs[i], k) + nck[i:i + 1, :]
            if masked:
                r = lax.broadcasted_iota(jnp.int32, (tq, tq), 0)
                c = lax.broadcasted_iota(jnp.int32, (tq, tq), 1)
                s = jnp.where(r >= c, s, -jnp.inf)
            out.extend(_softmax_step(s, v, *carry[3 * i:3 * i + 3]))
        return tuple(out)

    init = (jnp.full((tq, 1), -jnp.inf, F32), jnp.zeros((tq, 1), F32), jnp.zeros((tq, LANES), F32)) * 2
    carry = lax.fori_loop(0, qi, lambda kj, cr: tile(kj, cr, False), init)
    m1, l1, a1, m2, l2, a2 = tile(qi, carry, True)
    o_ref[0] = jnp.where(lo, a1 / l1, a2 / l2).astype(BF16)


def _fox_attn_prompt(qb, kb, vb, nck):
    B, T, D = qb.shape
    tq = ATT_TILE
    nh = FOX_HEADS // 2
    kern = functools.partial(_fox_attn_kernel, tq=tq)
    return pl.pallas_call(
        kern,
        grid=(B, nh, T // tq),
        in_specs=[
            pl.BlockSpec((1, tq, LANES), lambda b, h, i: (b, i, h)),
            pl.BlockSpec((1, T, LANES), lambda b, h, i: (b, 0, h)),
            pl.BlockSpec((1, T, LANES), lambda b, h, i: (b, 0, h)),
            pl.BlockSpec((1, 1, T // tq, 2, tq), lambda b, h, i: (b, h, 0, 0, 0)),
        ],
        out_specs=pl.BlockSpec((1, tq, LANES), lambda b, h, i: (b, i, h)),
        out_shape=jax.ShapeDtypeStruct((B, T, D), BF16),
        compiler_params=_cparams("arbitrary", "arbitrary", "arbitrary"),
        name="fox_attn_prompt",
    )(qb, kb, vb, nck)


def _page_specs(n_pages_step, page_shape):
    P = n_pages_step
    zeros = (0,) * len(page_shape)

    def mk(p):
        return pl.BlockSpec((1,) + tuple(page_shape), lambda b, j, pt: (pt[b, j * P + p],) + zeros)

    return [mk(p) for p in range(P)]


def _rows_softmax(s, m_sc, l_sc):
    m = m_sc[...]
    m_new = jnp.maximum(m, jnp.max(s, axis=-1, keepdims=True))
    alpha = jnp.exp2(m - m_new)
    p = jnp.exp2(s - m_new)
    l_sc[...] = alpha * l_sc[...] + jnp.sum(p, axis=-1, keepdims=True)
    m_sc[...] = m_new
    return alpha, p.astype(BF16)


def _diff_sample_kernel(pt_ref, q_ref, *rest, P, nj, dec_seq, lam_init):
    kpages = rest[:P]
    vpages = rest[P:2 * P]
    kn_ref, vn_ref, lq1, lk1, lq2, lk2, on_ref, o_ref, qm_sc, mask_sc, m_sc, l_sc, acc_sc = rest[2 * P:]
    j = pl.program_id(1)
    nr = 2 * dec_seq
    nrow = DIFF_HEADS * nr
    ncol = PAGE_SIZE * DIFF_HEADS

    @pl.when(j == 0)
    def _():
        r = lax.broadcasted_iota(jnp.int32, (nrow, LANES), 0)
        c = lax.broadcasted_iota(jnp.int32, (nrow, LANES), 1)
        q = q_ref[0]
        qm_sc[...] = jnp.where(c // DIFF_HD == (r % nr) // dec_seq, q, jnp.zeros_like(q))
        rr = lax.broadcasted_iota(jnp.int32, (nrow, ncol), 0)
        cc = lax.broadcasted_iota(jnp.int32, (nrow, ncol), 1)
        mask_sc[...] = jnp.where(cc % DIFF_HEADS == rr // nr, 0.0, -jnp.inf)
        m_sc[...] = jnp.full_like(m_sc, -jnp.inf)
        l_sc[...] = jnp.zeros_like(l_sc)
        acc_sc[...] = jnp.zeros_like(acc_sc)

    def flat(pg):
        return pg[0].reshape(ncol, LANES).astype(BF16)

    qm = qm_sc[...]
    ss = [_dot_nt(qm, flat(pg)) + mask_sc[...] for pg in kpages]
    m = m_sc[...]
    m_new = m
    for s in ss:
        m_new = jnp.maximum(m_new, jnp.max(s, axis=-1, keepdims=True))
    alpha = jnp.exp2(m - m_new)
    l = alpha * l_sc[...]
    acc = alpha * acc_sc[...]
    for s, pg in zip(ss, vpages):
        p = jnp.exp2(s - m_new)
        l = l + jnp.sum(p, axis=-1, keepdims=True)
        acc = acc + _dot(p.astype(BF16), flat(pg))
    m_sc[...] = m_new
    l_sc[...] = l
    acc_sc[...] = acc

    @pl.when(j == nj - 1)
    def _():
        npad = kn_ref.shape[1]
        rr = lax.broadcasted_iota(jnp.int32, (nrow, npad), 0)
        cc = lax.broadcasted_iota(jnp.int32, (nrow, npad), 1)
        allowed = jnp.logical_and(cc % DIFF_HEADS == rr // nr, cc // DIFF_HEADS <= rr % dec_seq)
        sn = jnp.where(allowed, _dot_nt(qm, kn_ref[0]), -jnp.inf)
        alpha_n, pn = _rows_softmax(sn, m_sc, l_sc)
        o = (alpha_n * acc + _dot(pn, vn_ref[0])) / l_sc[...]
        lam = _diff_lambda(lq1, lk1, lq2, lk2, lam_init)
        for h in range(DIFF_HEADS):
            od = o[h * nr:h * nr + dec_seq] - lam * o[h * nr + dec_seq:(h + 1) * nr]
            o_ref[0, :, h * LANES:(h + 1) * LANES] = (
                _rms_rows(od, on_ref[...]) * (1.0 - lam_init)).astype(BF16)


def _diff_attn_sample(page_table, q, cache_k, cache_v, kn, vn, lq1, lk1, lq2, lk2, onorm, lam_init,
                      dec_seq):
    nb, npages = page_table.shape
    P = PAGES_PER_STEP
    nj = npages // P
    nrow = q.shape[1]
    kern = functools.partial(_diff_sample_kernel, P=P, nj=nj, dec_seq=dec_seq, lam_init=lam_init)
    per_seq = lambda a: pl.BlockSpec((1,) + a.shape[1:], lambda b, j, pt: (b,) + (0,) * (a.ndim - 1))
    vec = pl.BlockSpec((1, DIFF_HD), lambda b, j, pt: (0, 0))
    page = cache_k.shape[1:]
    grid_spec = pltpu.PrefetchScalarGridSpec(
        num_scalar_prefetch=1,
        grid=(nb, nj),
        in_specs=([per_seq(q)] + _page_specs(P, page) + _page_specs(P, page)
                  + [per_seq(kn), per_seq(vn), vec, vec, vec, vec,
                     pl.BlockSpec((1, LANES), lambda b, j, pt: (0, 0))]),
        out_specs=pl.BlockSpec((1, dec_seq, D_MODEL), lambda b, j, pt: (b, 0, 0)),
        scratch_shapes=[pltpu.VMEM((nrow, LANES), BF16), pltpu.VMEM((nrow, PAGE_SIZE * DIFF_HEADS), F32),
                        pltpu.VMEM((nrow, 1), F32), pltpu.VMEM((nrow, 1), F32),
                        pltpu.VMEM((nrow, LANES), F32)],
    )
    return pl.pallas_call(
        kern,
        grid_spec=grid_spec,
        out_shape=jax.ShapeDtypeStruct((nb, dec_seq, D_MODEL), BF16),
        compiler_params=_cparams("arbitrary", "arbitrary"),
        name="diff_attn_sample",
    )(page_table, q, *([cache_k] * P), *([cache_v] * P), kn, vn, lq1, lk1, lq2, lk2, onorm)


def _fox_sample_kernel(pt_ref, q_ref, *rest, P, nj, dec_seq):
    kpages = rest[:P]
    vpages = rest[P:2 * P]
    lpages = rest[2 * P:3 * P]
    kn_ref, vn_ref, ncn_ref, o_ref, qt_sc, run_sc, m_sc, l_sc, acc_sc = rest[3 * P:]
    j = pl.program_id(1)
    nrow = dec_seq * FOX_HEADS

    @pl.when(j == 0)
    def _():
        rq = lax.broadcasted_iota(jnp.int32, (nrow, D_MODEL), 0)
        cq = lax.broadcasted_iota(jnp.int32, (nrow, D_MODEL), 1)
        q = q_ref[0]
        qt_sc[...] = jnp.where(cq // FOX_HD == rq % FOX_HEADS, q, jnp.zeros_like(q))
        run_sc[...] = jnp.zeros_like(run_sc)
        m_sc[...] = jnp.full_like(m_sc, -jnp.inf)
        l_sc[...] = jnp.zeros_like(l_sc)
        acc_sc[...] = jnp.zeros_like(acc_sc)

    r = lax.broadcasted_iota(jnp.int32, (PAGE_SIZE, PAGE_SIZE), 0)
    c = lax.broadcasted_iota(jnp.int32, (PAGE_SIZE, PAGE_SIZE), 1)
    upper = jnp.where(r <= c, 1.0, 0.0).astype(BF16)
    run = run_sc[...]
    cums = []
    for pg in lpages:
        hi, mid, lo = _split3(pg[0])
        local = _dot(hi, upper) + _dot(mid, upper) + _dot(lo, upper)
        cums.append(local + run)
        run = run + local[:, PAGE_SIZE - 1:PAGE_SIZE]
    run_sc[...] = run
    nck = jnp.concatenate(cums, axis=1) * (-LOG2E)

    def flat_t(pages):
        return jnp.concatenate([pg[0].reshape(D_MODEL, PAGE_SIZE).astype(BF16) for pg in pages], axis=1)

    qt = qt_sc[...]
    s = _dot(qt, flat_t(kpages)) + jnp.concatenate([nck] * dec_seq, axis=0)
    alpha, p = _rows_softmax(s, m_sc, l_sc)
    acc = alpha * acc_sc[...] + _dot_nt(p, flat_t(vpages))
    acc_sc[...] = acc

    @pl.when(j == nj - 1)
    def _():
        npad = kn_ref.shape[1]
        ncn = run * (-LOG2E) + ncn_ref[0]
        sn = _dot_nt(qt, kn_ref[0]) + jnp.concatenate([ncn] * dec_seq, axis=0)
        rr = lax.broadcasted_iota(jnp.int32, (nrow, npad), 0)
        cc = lax.broadcasted_iota(jnp.int32, (nrow, npad), 1)
        sn = jnp.where(cc <= rr // FOX_HEADS, sn, -jnp.inf)
        alpha_n, pn = _rows_softmax(sn, m_sc, l_sc)
        o = (alpha_n * acc + _dot(pn, vn_ref[0])) / l_sc[...]
        ro = lax.broadcasted_iota(jnp.int32, (nrow, D_MODEL), 0)
        co = lax.broadcasted_iota(jnp.int32, (nrow, D_MODEL), 1)
        o = jnp.where(co // FOX_HD == ro % FOX_HEADS, o, 0.0)
        rows = [jnp.sum(o[t * FOX_HEADS:(t + 1) * FOX_HEADS], axis=0, keepdims=True)
                for t in range(dec_seq)]
        o_ref[0] = jnp.concatenate(rows, axis=0).astype(BF16)


def _fox_attn_sample(page_table, q, cache_kt, cache_vt, cache_lt, kn, vn, ncn, dec_seq):
    nb, npages = page_table.shape
    P = PAGES_PER_STEP
    nj = npages // P
    nrow = q.shape[1]
    kern = functools.partial(_fox_sample_kernel, P=P, nj=nj, dec_seq=dec_seq)
    per_seq = lambda a: pl.BlockSpec((1,) + a.shape[1:], lambda b, j, pt: (b,) + (0,) * (a.ndim - 1))
    grid_spec = pltpu.PrefetchScalarGridSpec(
        num_scalar_prefetch=1,
        grid=(nb, nj),
        in_specs=([per_seq(q)] + _page_specs(P, cache_kt.shape[1:]) + _page_specs(P, cache_vt.shape[1:])
                  + _page_specs(P, cache_lt.shape[1:]) + [per_seq(kn), per_seq(vn), per_seq(ncn)]),
        out_specs=pl.BlockSpec((1, dec_seq, D_MODEL), lambda b, j, pt: (b, 0, 0)),
        scratch_shapes=[pltpu.VMEM((nrow, D_MODEL), BF16), pltpu.VMEM((FOX_HEADS, 1), F32),
                        pltpu.VMEM((nrow, 1), F32), pltpu.VMEM((nrow, 1), F32),
                        pltpu.VMEM((nrow, D_MODEL), F32)],
    )
    return pl.pallas_call(
        kern,
        grid_spec=grid_spec,
        out_shape=jax.ShapeDtypeStruct((nb, dec_seq, D_MODEL), BF16),
        compiler_params=_cparams("arbitrary", "arbitrary"),
        name="fox_attn_sample",
    )(page_table, q, *([cache_kt] * P), *([cache_vt] * P), *([cache_lt] * P), kn, vn, ncn)


def _sgu_kernel(x_ref, g_ref, win_ref, vn_ref, wmix_ref, bias_ref, wout_ref, *rest,
                tq, shift, emit_v):
    if emit_v:
        y_ref, v_ref, um = rest
    else:
        y_ref, um = rest
    x = x_ref[0]
    hb = _rms_rows(x, g_ref[...]).astype(BF16)

    def gelu(a):
        return 0.5 * a * (1.0 + lax.erf(a * (1.0 / math.sqrt(2.0))))

    v = _rms_rows(gelu(_dot(hb, win_ref[:, SG_WIDTH:2 * SG_WIDTH])), vn_ref[...])
    if emit_v:
        v_ref[0] = v
    vb = v.astype(BF16)
    u = gelu(_dot(hb, win_ref[:, 0:SG_WIDTH]))
    mask = _time_mask(SG_CHUNK, shift)
    for g in range(SG_GROUPS):
        w = jnp.where(mask, wmix_ref[g], 0.0).astype(BF16)
        cs = slice(g * SG_GD, (g + 1) * SG_GD)
        for ch in range(tq // SG_CHUNK):
            rs = slice(ch * SG_CHUNK, (ch + 1) * SG_CHUNK)
            mixed = _dot(w, vb[rs, cs]) + bias_ref[:, g:g + 1]
            um[rs, cs] = (u[rs, cs] * mixed).astype(BF16)
    y_ref[0] = x + _dot(um[...], wout_ref[...])


def _sgu_layer(x, g, win_bf, vnorm, wmix, bias, wout_bf, *, shift, emit_v):
    B, T, D = x.shape
    tq = min(T, SGU_TILE)
    row = pl.BlockSpec((1, tq, D), lambda b, t: (b, t, 0))
    out_specs = [row]
    out_shape = [jax.ShapeDtypeStruct((B, T, D), F32)]
    if emit_v:
        out_specs.append(pl.BlockSpec((1, tq, SG_WIDTH), lambda b, t: (b, t, 0)))
        out_shape.append(jax.ShapeDtypeStruct((B, T, SG_WIDTH), F32))
    kern = functools.partial(_sgu_kernel, tq=tq, shift=shift, emit_v=emit_v)
    return pl.pallas_call(
        kern,
        grid=(B, T // tq),
        in_specs=[row, _const_spec((1, D)), _const_spec(win_bf.shape), _const_spec((1, SG_WIDTH)),
                  _const_spec(wmix.shape), _const_spec(bias.shape), _const_spec(wout_bf.shape)],
        out_specs=out_specs,
        out_shape=out_shape,
        scratch_shapes=[pltpu.VMEM((tq, SG_WIDTH), BF16)],
        compiler_params=_cparams("arbitrary", "arbitrary"),
        name="sgu_mixer",
    )(x, g, win_bf, vnorm, wmix, bias, wout_bf)


def _lambda_init(layer_idx):
    return 0.8 - 0.6 * math.exp(-0.3 * layer_idx)


def _rope_tables(pos):
    half = ROPE_DIM // 2
    inv = ROPE_THETA ** (-jnp.arange(half, dtype=F32) / half)
    ang = pos.astype(F32)[:, None] * inv[None, :]
    lane = jnp.arange(LANES) % SEG
    idx = lane % half
    cos = jnp.where(lane[None, :] < ROPE_DIM, jnp.cos(ang)[:, idx], 1.0)
    sin = jnp.sin(ang)[:, idx]
    s1 = jnp.where(lane[None, :] < half, -sin, 0.0)
    s2 = jnp.where(jnp.logical_and(lane[None, :] >= half, lane[None, :] < ROPE_DIM), sin, 0.0)
    return cos.astype(F32), s1.astype(F32), s2.astype(F32)


def _to_time_major(a):
    a = jnp.swapaxes(a, 0, 1)
    return a.reshape(1, a.shape[0] * a.shape[1], *a.shape[2:])


def _from_time_major(a, nb):
    a = a.reshape(a.shape[1] // nb, nb, *a.shape[2:])
    return jnp.swapaxes(a, 0, 1)


def kernel(x_prompt, x_sample, state_pool, cache_diff_k, cache_diff_v, cache_fox_k, cache_fox_v,
           cache_fox_logf, state_ffn_conv, page_table, norm_mix, norm_ffn, pool_w, pool_scale,
           diff_w_qkv, diff_q_norm, diff_k_norm, diff_lq1, diff_lk1, diff_lq2, diff_lk2, diff_o_norm,
           diff_w_o, fox_w_qkvf, fox_b_f, fox_q_norm, fox_k_norm, fox_w_o, sgu_w_in, sgu_v_norm,
           sgu_w_s, sgu_b_s, sgu_w_out, ffn_w_up, ffn_conv_w, ffn_conv_b, ffn_w_down):
    bp, seq, _ = x_prompt.shape
    nb, dec_seq, _ = x_sample.shape
    n_pool = cache_diff_k.shape[0]
    past_len = page_table.shape[1] * PAGE_SIZE
    row2 = lambda a: a.reshape(1, -1).astype(F32)

    pool_w_bf = pool_w.astype(BF16)
    wu = ffn_w_up.astype(BF16).reshape(DEPTH, D_MODEL, 2 * FFN_NCH, FFN_CK).transpose(0, 2, 1, 3)
    wd = ffn_w_down.astype(BF16)
    cw = ffn_conv_w.reshape(DEPTH, FFN_CONV, 2 * FFN_NCH, FFN_CK).transpose(0, 2, 1, 3)
    cb = ffn_conv_b.reshape(DEPTH, 2 * FFN_NCH, 1, FFN_CK)
    seg_id = jnp.arange(D_MODEL) // SEG
    seg = (seg_id[:, None] == jnp.arange(LANES)[None, :]).astype(BF16)
    segt = seg.T
    diff_w_bf = diff_w_qkv.astype(BF16)
    diff_wo_bf = diff_w_o.astype(BF16)
    fox_w_bf = fox_w_qkvf[:, :3 * D_MODEL].astype(BF16)
    fox_wf_bf = jnp.pad(fox_w_qkvf[:, 3 * D_MODEL:], ((0, 0), (0, LANES - FOX_HEADS))).astype(BF16)
    fox_bf = jnp.pad(fox_b_f, (0, LANES - FOX_HEADS)).reshape(1, LANES)
    fox_wo_bf = fox_w_o.astype(BF16)
    tile_seg = lambda g: jnp.tile(g, NSEG).reshape(1, D_MODEL)
    sgu_win_bf = sgu_w_in.astype(BF16)
    sgu_wout_bf = sgu_w_out.astype(BF16)
    lvec = [row2(a) for a in (diff_lq1, diff_lk1, diff_lq2, diff_lk2)]
    onorm = row2(diff_o_norm)

    def trunk(x, *, shift, start, pool_prev, conv_prev, rope, sgu_mix, sgu_bias, attn_diff, attn_fox, emit_v):
        new = {}
        conv_new = []
        for i in range(DEPTH):
            g = row2(norm_mix[i])
            if i == 0:
                x, new["pool"] = _pool_layer(x, pool_prev, g, pool_w_bf, row2(pool_scale),
                                             shift=shift, start=start)
            elif i == 1:
                qb, kf, kb, vf, vb = _diff_proj_layer(x, g, diff_w_bf, seg, segt, tile_seg(diff_q_norm),
                                                      tile_seg(diff_k_norm), *rope)
                new["diff_k"], new["diff_v"] = kf, vf
                x = _oproj_layer(x, attn_diff(qb, kb, vb, _lambda_init(i)), diff_wo_bf)
            elif i == 2:
                qb, kf, kb, vf, vb, lf, nck = _fox_proj_layer(
                    x, g, fox_w_bf, fox_wf_bf, fox_bf, seg, segt, tile_seg(fox_q_norm),
                    tile_seg(fox_k_norm), shift=shift)
                new["fox_k"], new["fox_v"], new["fox_logf"] = kf, vf, lf[..., :FOX_HEADS]
                x = _oproj_layer(x, attn_fox(qb, kb, vb, nck[..., :FOX_HEADS]), fox_wo_bf)
            else:
                res = _sgu_layer(x, g, sgu_win_bf, row2(sgu_v_norm), sgu_mix, sgu_bias, sgu_wout_bf,
                                 shift=shift, emit_v=emit_v)
                x = res[0]
                if emit_v:
                    new["sgu_v"] = res[1]
            x, st = _ffn_layer(x, conv_prev[i], row2(norm_ffn[i]), wu[i], cw[i], cb[i], wd[i], shift=shift)
            conv_new.append(st)
        return x, new, conv_new

    halo_p = SUBLANES
    rope_p = _rope_tables(jnp.arange(seq, dtype=jnp.int32))

    def attn_fox_prompt(qb, kb, vb, nck):
        nck5 = nck.reshape(bp, seq // ATT_TILE, ATT_TILE, FOX_HEADS // 2, 2).transpose(0, 3, 1, 4, 2)
        return _fox_attn_prompt(qb, kb, vb, nck5)

    yp, newp, convp = trunk(
        x_prompt, shift=1, start=0,
        pool_prev=jnp.zeros((bp, 2 * SUBLANES, D_MODEL), F32),
        conv_prev=[jnp.zeros((bp, 2 * FFN_NCH, halo_p, FFN_CK), F32)] * DEPTH,
        rope=rope_p,
        sgu_mix=sgu_w_s, sgu_bias=sgu_b_s.T,
        attn_diff=lambda qb, kb, vb, li: _diff_attn_prompt(qb, kb, vb, *lvec, onorm, li),
        attn_fox=attn_fox_prompt, emit_v=False)

    rows = dec_seq * nb
    t_of_row = jnp.arange(rows, dtype=jnp.int32) // nb
    rope_s = _rope_tables(past_len + t_of_row)
    fk_cache = cache_fox_k.transpose(0, 2, 3, 1)
    fv_cache = cache_fox_v.transpose(0, 2, 3, 1)
    fl_cache = cache_fox_logf.transpose(0, 2, 1)
    npad = 2 * SUBLANES

    def seq_major(a):
        return _from_time_major(a, nb)

    def pad_new(a):
        return jnp.pad(seq_major(a), ((0, 0), (0, npad - dec_seq), (0, 0)))

    def attn_diff_sample(qb, kb, vb, li):
        q = seq_major(qb).reshape(nb, dec_seq, DIFF_HEADS, LANES).transpose(0, 2, 1, 3)
        q = jnp.tile(q, (1, 1, 2, 1)).reshape(nb, DIFF_HEADS * 2 * dec_seq, LANES)
        flat_new = lambda a: pad_new(a).reshape(nb, npad * DIFF_HEADS, LANES)
        o = _diff_attn_sample(page_table, q, cache_diff_k, cache_diff_v, flat_new(kb), flat_new(vb),
                              *lvec, onorm, li, dec_seq)
        return _to_time_major(o)

    def attn_fox_sample(qb, kb, vb, nck):
        ncn = jnp.pad(jnp.swapaxes(seq_major(nck), 1, 2), ((0, 0), (0, 0), (0, npad - dec_seq)))
        q = jnp.repeat(seq_major(qb), FOX_HEADS, axis=1)
        o = _fox_attn_sample(page_table, q, fk_cache, fv_cache, fl_cache, pad_new(kb), pad_new(vb), ncn,
                             dec_seq)
        return _to_time_major(o)

    halo_s = (FFN_CONV - 1) * nb
    conv_prev_s = [
        state_ffn_conv[i].reshape(nb, FFN_CONV - 1, 2 * FFN_NCH, FFN_CK).transpose(2, 1, 0, 3)
        .reshape(1, 2 * FFN_NCH, halo_s, FFN_CK) for i in range(DEPTH)]
    sgu_mix_s = jnp.repeat(jnp.repeat(sgu_w_s[:, :dec_seq, :dec_seq], nb, axis=1), nb, axis=2)
    sgu_bias_s = jnp.repeat(sgu_b_s[:, :dec_seq].T, nb, axis=0)
    ys, news, convs = trunk(
        _to_time_major(x_sample), shift=nb, start=past_len,
        pool_prev=_to_time_major(state_pool),
        conv_prev=conv_prev_s, rope=rope_s,
        sgu_mix=sgu_mix_s, sgu_bias=sgu_bias_s,
        attn_diff=attn_diff_sample, attn_fox=attn_fox_sample, emit_v=True)

    def conv_state_prompt(st):
        return st.transpose(0, 2, 1, 3).reshape(bp, FFN_CONV - 1, 2 * FFN_DIM)

    def conv_state_sample(st):
        return (st.reshape(2 * FFN_NCH, FFN_CONV - 1, nb, FFN_CK).transpose(2, 1, 0, 3)
                .reshape(nb, FFN_CONV - 1, 2 * FFN_DIM))

    hd4 = lambda a, h: a.reshape(a.shape[0], a.shape[1], h, D_MODEL // h)
    sm = seq_major
    return (
        yp, sm(ys),
        newp["pool"], sm(news["pool"]),
        hd4(newp["diff_k"], DIFF_HEADS), hd4(newp["diff_v"], DIFF_HEADS),
        hd4(sm(news["diff_k"]), DIFF_HEADS), hd4(sm(news["diff_v"]), DIFF_HEADS),
        hd4(newp["fox_k"], FOX_HEADS), hd4(newp["fox_v"], FOX_HEADS), newp["fox_logf"],
        hd4(sm(news["fox_k"]), FOX_HEADS), hd4(sm(news["fox_v"]), FOX_HEADS), sm(news["fox_logf"]),
        sm(news["sgu_v"]),
        jnp.stack([conv_state_prompt(s) for s in convp], axis=0),
        jnp.stack([conv_state_sample(s) for s in convs], axis=0),
    )
```

```python
import functools
import math

import jax
import jax.numpy as jnp
from jax import lax
from jax.experimental import pallas as pl
from jax.experimental.pallas import tpu as pltpu

F32 = jnp.float32
BF16 = jnp.bfloat16

D_MODEL = 1024
DEPTH = 4
PAGE_SIZE = 128
POOL_WINDOWS = (2, 4, 8, 16)
POOL_GD = D_MODEL // len(POOL_WINDOWS)
POOL_BUF = max(POOL_WINDOWS) - 1
DIFF_HEADS = 8
DIFF_HD = 64
FOX_HEADS = 16
FOX_HD = 64
SG_WIDTH = 2 * D_MODEL
SG_GROUPS = 8
SG_GD = SG_WIDTH // SG_GROUPS
SG_CHUNK = 128
FFN_DIM = 2816
FFN_CONV = 3
ROPE_THETA = 500000.0
ROPE_DIM = DIFF_HD // 4
NORM_EPS = 1e-6

LANES = 128
SUBLANES = 8
FFN_CK = 256
FFN_NCH = FFN_DIM // FFN_CK
SEG = 64
NSEG = D_MODEL // SEG
PAGES_PER_STEP = 8
ROW_TILE = 512
SGU_TILE = 256
ATT_TILE = 512
VMEM_LIMIT = 56 * 1024 * 1024
LOG2E = math.log2(math.e)


def _cparams(*sem):
    return pltpu.CompilerParams(dimension_semantics=sem, vmem_limit_bytes=VMEM_LIMIT)


def _rms_rows(x, g):
    ms = jnp.mean(x * x, axis=-1, keepdims=True)
    return x * lax.rsqrt(ms + NORM_EPS) * g


def _dot(a, b):
    return jnp.dot(a, b, preferred_element_type=F32)


def _dot_nt(a, b):
    return lax.dot_general(a, b, (((1,), (1,)), ((), ())), preferred_element_type=F32)


def _split3(a):
    hi = a.astype(BF16)
    r1 = a - hi.astype(F32)
    mid = r1.astype(BF16)
    lo = (r1 - mid.astype(F32)).astype(BF16)
    return hi, mid, lo


def _time_mask(n, shift):
    r = lax.broadcasted_iota(jnp.int32, (n, n), 0)
    c = lax.broadcasted_iota(jnp.int32, (n, n), 1)
    m = c <= r
    if shift > 1:
        m = jnp.logical_and(m, (r - c) % shift == 0)
    return m


def _const_spec(shape):
    nd = len(shape)
    return pl.BlockSpec(shape, lambda *_: (0,) * nd, pipeline_mode=pl.Buffered(1))


def _pool_kernel(x_ref, prev_ref, g_ref, w_ref, scale_ref, y_ref, st_ref, hext,
                 *, tq, shift, halo, start, nt):
    t = pl.program_id(1)

    @pl.when(t == 0)
    def _():
        hext[0:halo, :] = prev_ref[0]

    x = x_ref[0]
    h = _rms_rows(x, g_ref[...])
    hext[halo:halo + tq, :] = h
    row = lax.broadcasted_iota(jnp.int32, (tq, 1), 0)
    pos = start + (t * tq + row) // shift
    for g, win in enumerate(POOL_WINDOWS):
        c0 = g * POOL_GD
        hg = h[:, c0:c0 + POOL_GD]
        s = hg
        for j in range(1, win):
            o = halo - j * shift
            s = s + hext[o:o + tq, c0:c0 + POOL_GD]
        cnt = jnp.minimum(pos + 1, win).astype(F32)
        d = (s / cnt - hg).astype(BF16)
        y = _dot(d, w_ref[g])
        y_ref[0, :, c0:c0 + POOL_GD] = x[:, c0:c0 + POOL_GD] + y * scale_ref[:, c0:c0 + POOL_GD]

    n = POOL_BUF * shift

    @pl.when(t == nt - 1)
    def _():
        st_ref[0] = hext[halo + tq - n:halo + tq, :]

    if nt > 1:
        hext[0:halo, :] = hext[tq:tq + halo, :]


def _pool_layer(x, prev, g, w_bf, scale, *, shift, start):
    B, T, D = x.shape
    tq = min(T, ROW_TILE)
    nt = T // tq
    halo = prev.shape[1]
    n = POOL_BUF * shift
    kern = functools.partial(_pool_kernel, tq=tq, shift=shift, halo=halo, start=start, nt=nt)
    return pl.pallas_call(
        kern,
        grid=(B, nt),
        in_specs=[
            pl.BlockSpec((1, tq, D), lambda b, t: (b, t, 0)),
            pl.BlockSpec((1, halo, D), lambda b, t: (b, 0, 0)),
            _const_spec((1, D)),
            _const_spec(w_bf.shape),
            _const_spec((1, D)),
        ],
        out_specs=[
            pl.BlockSpec((1, tq, D), lambda b, t: (b, t, 0)),
            pl.BlockSpec((1, n, D), lambda b, t: (b, 0, 0)),
        ],
        out_shape=[jax.ShapeDtypeStruct((B, T, D), F32), jax.ShapeDtypeStruct((B, n, D), F32)],
        scratch_shapes=[pltpu.VMEM((halo + tq, D), F32)],
        compiler_params=_cparams("arbitrary", "arbitrary"),
        name="pool_mixer",
    )(x, prev, g, w_bf, scale)


def _ffn_kernel(x_ref, prev_ref, g_ref, wu_ref, cw_ref, cb_ref, wd_ref, y_ref, st_ref,
                hb, buf, act, *, tq, shift, halo, nt):
    t = pl.program_id(1)
    x = x_ref[0]
    hb[...] = _rms_rows(x, g_ref[...]).astype(BF16)

    @pl.when(t == 0)
    def _():
        buf[:, 0:halo, :] = prev_ref[0]

    def conv(idx):
        u = _dot(hb[...], wu_ref[idx])
        buf[idx, halo:halo + tq, :] = u
        w = cw_ref[idx]
        o2 = halo - 2 * shift
        o1 = halo - shift
        out = (cb_ref[idx] + w[0:1] * buf[idx, o2:o2 + tq, :]
               + w[1:2] * buf[idx, o1:o1 + tq, :] + w[2:3] * u)
        buf[idx, 0:halo, :] = buf[idx, tq:tq + halo, :]
        return out

    for c in range(FFN_NCH):
        ag = conv(c)
        av = conv(FFN_NCH + c)
        act[:, c * FFN_CK:(c + 1) * FFN_CK] = (ag / (1.0 + jnp.exp(-ag)) * av).astype(BF16)
    y_ref[0] = x + _dot(act[...], wd_ref[...])

    @pl.when(t == nt - 1)
    def _():
        st_ref[0] = buf[:, halo - 2 * shift:halo, :]


def _ffn_layer(x, prev, g, wu, cw, cb, wd, *, shift):
    B, T, D = x.shape
    tq = min(T, ROW_TILE)
    nt = T // tq
    halo = prev.shape[2]
    ns = (FFN_CONV - 1) * shift
    kern = functools.partial(_ffn_kernel, tq=tq, shift=shift, halo=halo, nt=nt)
    return pl.pallas_call(
        kern,
        grid=(B, nt),
        in_specs=[
            pl.BlockSpec((1, tq, D), lambda b, t: (b, t, 0)),
            pl.BlockSpec((1, 2 * FFN_NCH, halo, FFN_CK), lambda b, t: (b, 0, 0, 0)),
            _const_spec((1, D)),
            _const_spec(wu.shape),
            _const_spec(cw.shape),
            _const_spec(cb.shape),
            _const_spec(wd.shape),
        ],
        out_specs=[
            pl.BlockSpec((1, tq, D), lambda b, t: (b, t, 0)),
            pl.BlockSpec((1, 2 * FFN_NCH, ns, FFN_CK), lambda b, t: (b, 0, 0, 0)),
        ],
        out_shape=[jax.ShapeDtypeStruct((B, T, D), F32),
                   jax.ShapeDtypeStruct((B, 2 * FFN_NCH, ns, FFN_CK), F32)],
        scratch_shapes=[pltpu.VMEM((tq, D), BF16),
                        pltpu.VMEM((2 * FFN_NCH, halo + tq, FFN_CK), F32),
                        pltpu.VMEM((tq, FFN_DIM), BF16)],
        compiler_params=_cparams("arbitrary", "arbitrary"),
        name="conv_ffn",
    )(x, prev, g, wu, cw, cb, wd)


def _oproj_kernel(x_ref, o_ref, w_ref, y_ref):
    y_ref[0] = x_ref[0] + _dot(o_ref[0], w_ref[...])


def _oproj_layer(x, o_bf, w_bf):
    B, T, D = x.shape
    tq = min(T, ROW_TILE)
    K = o_bf.shape[-1]
    return pl.pallas_call(
        _oproj_kernel,
        grid=(B, T // tq),
        in_specs=[
            pl.BlockSpec((1, tq, D), lambda b, t: (b, t, 0)),
            pl.BlockSpec((1, tq, K), lambda b, t: (b, t, 0)),
            _const_spec(w_bf.shape),
        ],
        out_specs=pl.BlockSpec((1, tq, D), lambda b, t: (b, t, 0)),
        out_shape=jax.ShapeDtypeStruct((B, T, D), F32),
        compiler_params=_cparams("arbitrary", "arbitrary"),
        name="out_proj",
    )(x, o_bf, w_bf)


def _seg_norm(a, seg_ref, segt_ref, gain):
    ms = _dot((a * a).astype(BF16), seg_ref[...]) * (1.0 / SEG)
    r = lax.rsqrt(ms + NORM_EPS)
    hi = r.astype(BF16)
    lo = (r - hi.astype(F32)).astype(BF16)
    rf = _dot(jnp.concatenate([hi, lo], axis=1), segt_ref[...])
    return a * rf * gain


def _rope_slab(s, cos, sin_lo, sin_hi):
    return s * cos + pltpu.roll(s, LANES - ROPE_DIM // 2, 1) * sin_lo + pltpu.roll(s, ROPE_DIM // 2, 1) * sin_hi


def _diff_proj_kernel(x_ref, g_ref, w_ref, seg_ref, segt_ref, qg_ref, kg_ref, cos_ref, s1_ref, s2_ref,
                      qb_ref, kf_ref, kb_ref, vf_ref, vb_ref, *, scale):
    hb = _rms_rows(x_ref[0], g_ref[...]).astype(BF16)
    cos, s1, s2 = cos_ref[...], s1_ref[...], s2_ref[...]
    q = _seg_norm(_dot(hb, w_ref[:, 0:D_MODEL]), seg_ref, segt_ref, qg_ref[...])
    for c in range(D_MODEL // LANES):
        sl = slice(c * LANES, (c + 1) * LANES)
        qb_ref[0, :, sl] = (_rope_slab(q[:, sl], cos, s1, s2) * scale).astype(BF16)
    k = _seg_norm(_dot(hb, w_ref[:, D_MODEL:2 * D_MODEL]), seg_ref, segt_ref, kg_ref[...])
    for c in range(D_MODEL // LANES):
        sl = slice(c * LANES, (c + 1) * LANES)
        kr = _rope_slab(k[:, sl], cos, s1, s2)
        kf_ref[0, :, sl] = kr
        kb_ref[0, :, sl] = kr.astype(BF16)
    v = _dot(hb, w_ref[:, 2 * D_MODEL:3 * D_MODEL])
    vf_ref[0] = v
    vb_ref[0] = v.astype(BF16)


def _diff_proj_layer(x, g, w_bf, seg, segt, qg, kg, cos, s1, s2):
    B, T, D = x.shape
    tq = min(T, ROW_TILE)
    row = pl.BlockSpec((1, tq, D), lambda b, t: (b, t, 0))
    tab = pl.BlockSpec((tq, LANES), lambda b, t: (t, 0))
    kern = functools.partial(_diff_proj_kernel, scale=DIFF_HD ** -0.5 * LOG2E)
    return pl.pallas_call(
        kern,
        grid=(B, T // tq),
        in_specs=[row, _const_spec((1, D)), _const_spec(w_bf.shape), _const_spec(seg.shape),
                  _const_spec(segt.shape), _const_spec((1, D)), _const_spec((1, D)), tab, tab, tab],
        out_specs=[row, row, row, row, row],
        out_shape=[jax.ShapeDtypeStruct((B, T, D), BF16), jax.ShapeDtypeStruct((B, T, D), F32),
                   jax.ShapeDtypeStruct((B, T, D), BF16), jax.ShapeDtypeStruct((B, T, D), F32),
                   jax.ShapeDtypeStruct((B, T, D), BF16)],
        compiler_params=_cparams("arbitrary", "arbitrary"),
        name="diff_qkv_proj",
    )(x, g, w_bf, seg, segt, qg, kg, cos, s1, s2)


def _fox_proj_kernel(x_ref, g_ref, w_ref, wf_ref, bf_ref, seg_ref, segt_ref, qg_ref, kg_ref,
                     qb_ref, kf_ref, kb_ref, vf_ref, vb_ref, lf_ref, nck_ref, run,
                     *, tq, shift, scale):
    t = pl.program_id(1)

    @pl.when(t == 0)
    def _():
        run[...] = jnp.zeros_like(run)

    hb = _rms_rows(x_ref[0], g_ref[...]).astype(BF16)
    q = _seg_norm(_dot(hb, w_ref[:, 0:D_MODEL]), seg_ref, segt_ref, qg_ref[...])
    qb_ref[0] = (q * scale).astype(BF16)
    k = _seg_norm(_dot(hb, w_ref[:, D_MODEL:2 * D_MODEL]), seg_ref, segt_ref, kg_ref[...])
    kf_ref[0] = k
    kb_ref[0] = k.astype(BF16)
    v = _dot(hb, w_ref[:, 2 * D_MODEL:3 * D_MODEL])
    vf_ref[0] = v
    vb_ref[0] = v.astype(BF16)
    z = _dot(hb, wf_ref[...]) + bf_ref[...]
    lf = jnp.minimum(z, 0.0) - jnp.log1p(jnp.exp(-jnp.abs(z)))
    lf_ref[0] = lf
    m = jnp.where(_time_mask(tq, shift), 1.0, 0.0).astype(BF16)
    hi, mid, lo = _split3(lf)
    cum = _dot(m, hi) + _dot(m, mid) + _dot(m, lo) + run[...]
    nck_ref[0] = cum * (-LOG2E)
    run[...] = cum[tq - 1:tq, :]


def _fox_proj_layer(x, g, w_bf, wf_bf, bf, seg, segt, qg, kg, *, shift):
    B, T, D = x.shape
    tq = min(T, ROW_TILE)
    row = pl.BlockSpec((1, tq, D), lambda b, t: (b, t, 0))
    nar = pl.BlockSpec((1, tq, LANES), lambda b, t: (b, t, 0))
    kern = functools.partial(_fox_proj_kernel, tq=tq, shift=shift, scale=FOX_HD ** -0.5 * LOG2E)
    return pl.pallas_call(
        kern,
        grid=(B, T // tq),
        in_specs=[row, _const_spec((1, D)), _const_spec(w_bf.shape), _const_spec(wf_bf.shape),
                  _const_spec((1, LANES)), _const_spec(seg.shape), _const_spec(segt.shape),
                  _const_spec((1, D)), _const_spec((1, D))],
        out_specs=[row, row, row, row, row, nar, nar],
        out_shape=[jax.ShapeDtypeStruct((B, T, D), BF16), jax.ShapeDtypeStruct((B, T, D), F32),
                   jax.ShapeDtypeStruct((B, T, D), BF16), jax.ShapeDtypeStruct((B, T, D), F32),
                   jax.ShapeDtypeStruct((B, T, D), BF16), jax.ShapeDtypeStruct((B, T, LANES), F32),
                   jax.ShapeDtypeStruct((B, T, LANES), F32)],
        scratch_shapes=[pltpu.VMEM((1, LANES), F32)],
        compiler_params=_cparams("arbitrary", "arbitrary"),
        name="fox_qkvf_proj",
    )(x, g, w_bf, wf_bf, bf, seg, segt, qg, kg)


def _diff_lambda(lq1, lk1, lq2, lk2, lam_init):
    a = jnp.sum(lq1[...] * lk1[...], axis=-1, keepdims=True)
    b = jnp.sum(lq2[...] * lk2[...], axis=-1, keepdims=True)
    return jnp.exp(a) - jnp.exp(b) + lam_init


ATT_CHUNK = 32


def _softmax_tile_t(s_ref, p_ref, vt, m, l, acc, masked):
    tk, tq = s_ref.shape
    nch = tk // ATT_CHUNK

    def chunk(c):
        blk = s_ref[c * ATT_CHUNK:(c + 1) * ATT_CHUNK, :]
        if masked:
            key = lax.broadcasted_iota(jnp.int32, (ATT_CHUNK, tq), 0) + c * ATT_CHUNK
            qry = lax.broadcasted_iota(jnp.int32, (ATT_CHUNK, tq), 1)
            blk = jnp.where(key <= qry, blk, -jnp.inf)
        return blk

    def fold(x, op):
        return op(x.reshape(ATT_CHUNK // SUBLANES, SUBLANES, tq), axis=0)

    mx = fold(chunk(0), jnp.max)
    for c in range(1, nch):
        mx = jnp.maximum(mx, fold(chunk(c), jnp.max))
    m_new = jnp.maximum(m, jnp.max(mx, axis=0, keepdims=True))
    alpha = jnp.exp2(m - m_new)
    ls = None
    for c in range(nch):
        p = jnp.exp2(chunk(c) - m_new)
        ps = fold(p, jnp.sum)
        ls = ps if ls is None else ls + ps
        p_ref[c * ATT_CHUNK:(c + 1) * ATT_CHUNK, :] = p.astype(BF16)
    l = alpha * l + jnp.sum(ls, axis=0, keepdims=True)
    acc = alpha * acc + _dot(vt, p_ref[...])
    return m_new, l, acc


def _stage_vt(v_ref, vt_sc, tq):
    for j in range(vt_sc.shape[0]):
        vt_sc[j] = v_ref[0, j * tq:(j + 1) * tq, :].astype(F32).T.astype(BF16)


def _flash_tiles(qi, put, consume, init, slot_a, slot_b):
    put(0, slot_a)

    def pair(a, stats):
        put(2 * a + 1, slot_b)
        stats = consume(2 * a, slot_a, stats, False)
        put(2 * a + 2, slot_a)
        return consume(2 * a + 1, slot_b, stats, False)

    stats = lax.fori_loop(0, qi // 2, pair, init)

    def last_in_a():
        return consume(qi, slot_a, stats, True)

    def last_in_b():
        put(qi, slot_b)
        return consume(qi, slot_b, consume(qi - 1, slot_a, stats, False), True)

    return lax.cond(qi % 2 == 0, last_in_a, last_in_b)


def _diff_attn_kernel(q_ref, k_ref, v_ref, lq1, lk1, lq2, lk2, on_ref, o_ref, vt_sc, sa_sc, sb_sc, p_sc,
                      *, tq, lam_init):
    qi = pl.program_id(2)

    @pl.when(qi == 0)
    def _():
        _stage_vt(v_ref, vt_sc, tq)

    q = q_ref[0]
    lo = lax.broadcasted_iota(jnp.int32, (1, LANES), 1) < DIFF_HD
    zero = jnp.zeros_like(q)
    qs = (jnp.where(lo, q, zero), jnp.where(lo, zero, q))

    def put(kj, slot):
        k = k_ref[0, pl.ds(pl.multiple_of(kj * tq, tq), tq), :]
        for i in range(2):
            slot[i] = _dot_nt(k, qs[i])

    def consume(kj, slot, stats, masked):
        vt = vt_sc[kj]
        out = []
        for i in range(2):
            out.extend(_softmax_tile_t(slot.at[i], p_sc.at[i], vt, *stats[3 * i:3 * i + 3], masked))
        return tuple(out)

    init = (jnp.full((1, tq), -jnp.inf, F32), jnp.zeros((1, tq), F32), jnp.zeros((LANES, tq), F32)) * 2
    m1, l1, a1, m2, l2, a2 = _flash_tiles(qi, put, consume, init, sa_sc, sb_sc)
    lam = _diff_lambda(lq1, lk1, lq2, lk2, lam_init)
    o = (a1 / l1 - lam * (a2 / l2)).T
    o_ref[0] = (_rms_rows(o, on_ref[...]) * (1.0 - lam_init)).astype(BF16)


def _attn_scratch(T, tq):
    return [pltpu.VMEM((T // tq, LANES, tq), BF16), pltpu.VMEM((2, tq, tq), F32),
            pltpu.VMEM((2, tq, tq), F32), pltpu.VMEM((2, tq, tq), BF16)]


def _diff_attn_prompt(qb, kb, vb, lq1, lk1, lq2, lk2, onorm, lam_init):
    B, T, D = qb.shape
    tq = ATT_TILE
    kern = functools.partial(_diff_attn_kernel, tq=tq, lam_init=lam_init)
    vec = _const_spec((1, DIFF_HD))
    return pl.pallas_call(
        kern,
        grid=(B, DIFF_HEADS, T // tq),
        in_specs=[
            pl.BlockSpec((1, tq, LANES), lambda b, h, i: (b, i, h)),
            pl.BlockSpec((1, T, LANES), lambda b, h, i: (b, 0, h)),
            pl.BlockSpec((1, T, LANES), lambda b, h, i: (b, 0, h)),
            vec, vec, vec, vec, _const_spec((1, LANES)),
        ],
        out_specs=pl.BlockSpec((1, tq, LANES), lambda b, h, i: (b, i, h)),
        out_shape=jax.ShapeDtypeStruct((B, T, D), BF16),
        scratch_shapes=_attn_scratch(T, tq),
        compiler_params=_cparams("arbitrary", "arbitrary", "arbitrary"),
        name="diff_attn_prompt",
    )(qb, kb, vb, lq1, lk1, lq2, lk2, onorm)


def _fox_attn_kernel(q_ref, k_ref, v_ref, nck_ref, o_ref, vt_sc, sa_sc, sb_sc, p_sc, *, tq):
    qi = pl.program_id(2)

    @pl.when(qi == 0)
    def _():
        _stage_vt(v_ref, vt_sc, tq)

    q = q_ref[0]
    lo = lax.broadcasted_iota(jnp.int32, (1, LANES), 1) < FOX_HD
    zero = jnp.zeros_like(q)
    qs = (jnp.where(lo, q, zero), jnp.where(lo, zero, q))

    def put(kj, slot):
        off = pl.multiple_of(kj * tq, tq)
        k = k_ref[0, pl.ds(off, tq), :]
        nck = nck_ref[0, 0, pl.ds(off, tq), :]
        for i in range(2):
            slot[i] = _dot_nt(k, qs[i]) + nck[:, i:i + 1]

    def consume(kj, slot, stats, masked):
        vt = vt_sc[kj]
        out = []
        for i in range(2):
            out.extend(_softmax_tile_t(slot.at[i], p_sc.at[i], vt[i * FOX_HD:(i + 1) * FOX_HD],
                                       *stats[3 * i:3 * i + 3], masked))
        return tuple(out)

    init = (jnp.full((1, tq), -jnp.inf, F32), jnp.zeros((1, tq), F32), jnp.zeros((FOX_HD, tq), F32)) * 2
    m1, l1, a1, m2, l2, a2 = _flash_tiles(qi, put, consume, init, sa_sc, sb_sc)
    o_ref[0] = jnp.concatenate([a1 / l1, a2 / l2], axis=0).T.astype(BF16)


def _fox_attn_prompt(qb, kb, vb, nck):
    B, T, D = qb.shape
    tq = ATT_TILE
    nh = FOX_HEADS // 2
    kern = functools.partial(_fox_attn_kernel, tq=tq)
    return pl.pallas_call(
        kern,
        grid=(B, nh, T // tq),
        in_specs=[
            pl.BlockSpec((1, tq, LANES), lambda b, h, i: (b, i, h)),
            pl.BlockSpec((1, T, LANES), lambda b, h, i: (b, 0, h)),
            pl.BlockSpec((1, T, LANES), lambda b, h, i: (b, 0, h)),
            pl.BlockSpec((1, 1, T, 2), lambda b, h, i: (b, h, 0, 0)),
        ],
        out_specs=pl.BlockSpec((1, tq, LANES), lambda b, h, i: (b, i, h)),
        out_shape=jax.ShapeDtypeStruct((B, T, D), BF16),
        scratch_shapes=_attn_scratch(T, tq),
        compiler_params=_cparams("arbitrary", "arbitrary", "arbitrary"),
        name="fox_attn_prompt",
    )(qb, kb, vb, nck)


def _page_specs(n_pages_step, page_shape):
    P = n_pages_step
    zeros = (0,) * len(page_shape)

    def mk(p):
        return pl.BlockSpec((1,) + tuple(page_shape), lambda b, j, pt: (pt[b, j * P + p],) + zeros)

    return [mk(p) for p in range(P)]


def _rows_softmax(s, m_sc, l_sc):
    m = m_sc[...]
    m_new = jnp.maximum(m, jnp.max(s, axis=-1, keepdims=True))
    alpha = jnp.exp2(m - m_new)
    p = jnp.exp2(s - m_new)
    l_sc[...] = alpha * l_sc[...] + jnp.sum(p, axis=-1, keepdims=True)
    m_sc[...] = m_new
    return alpha, p.astype(BF16)


def _diff_sample_kernel(pt_ref, q_ref, *rest, P, nj, dec_seq, lam_init):
    kpages = rest[:P]
    vpages = rest[P:2 * P]
    kn_ref, vn_ref, lq1, lk1, lq2, lk2, on_ref, o_ref, qm_sc, mask_sc, m_sc, l_sc, acc_sc = rest[2 * P:]
    j = pl.program_id(1)
    nr = 2 * dec_seq
    nrow = DIFF_HEADS * nr
    ncol = PAGE_SIZE * DIFF_HEADS

    @pl.when(j == 0)
    def _():
        r = lax.broadcasted_iota(jnp.int32, (nrow, LANES), 0)
        c = lax.broadcasted_iota(jnp.int32, (nrow, LANES), 1)
        q = q_ref[0]
        qm_sc[...] = jnp.where(c // DIFF_HD == (r % nr) // dec_seq, q, jnp.zeros_like(q))
        rr = lax.broadcasted_iota(jnp.int32, (nrow, ncol), 0)
        cc = lax.broadcasted_iota(jnp.int32, (nrow, ncol), 1)
        mask_sc[...] = jnp.where(cc % DIFF_HEADS == rr // nr, 0.0, -jnp.inf)
        m_sc[...] = jnp.full_like(m_sc, -jnp.inf)
        l_sc[...] = jnp.zeros_like(l_sc)
        acc_sc[...] = jnp.zeros_like(acc_sc)

    def flat(pg):
        return pg[0].reshape(ncol, LANES).astype(BF16)

    qm = qm_sc[...]
    ss = [_dot_nt(qm, flat(pg)) + mask_sc[...] for pg in kpages]
    m = m_sc[...]
    m_new = m
    for s in ss:
        m_new = jnp.maximum(m_new, jnp.max(s, axis=-1, keepdims=True))
    alpha = jnp.exp2(m - m_new)
    l = alpha * l_sc[...]
    acc = alpha * acc_sc[...]
    for s, pg in zip(ss, vpages):
        p = jnp.exp2(s - m_new)
        l = l + jnp.sum(p, axis=-1, keepdims=True)
        acc = acc + _dot(p.astype(BF16), flat(pg))
    m_sc[...] = m_new
    l_sc[...] = l
    acc_sc[...] = acc

    @pl.when(j == nj - 1)
    def _():
        npad = kn_ref.shape[1]
        rr = lax.broadcasted_iota(jnp.int32, (nrow, npad), 0)
        cc = lax.broadcasted_iota(jnp.int32, (nrow, npad), 1)
        allowed = jnp.logical_and(cc % DIFF_HEADS == rr // nr, cc // DIFF_HEADS <= rr % dec_seq)
        sn = jnp.where(allowed, _dot_nt(qm, kn_ref[0]), -jnp.inf)
        alpha_n, pn = _rows_softmax(sn, m_sc, l_sc)
        o = (alpha_n * acc + _dot(pn, vn_ref[0])) / l_sc[...]
        lam = _diff_lambda(lq1, lk1, lq2, lk2, lam_init)
        for h in range(DIFF_HEADS):
            od = o[h * nr:h * nr + dec_seq] - lam * o[h * nr + dec_seq:(h + 1) * nr]
            o_ref[0, :, h * LANES:(h + 1) * LANES] = (
                _rms_rows(od, on_ref[...]) * (1.0 - lam_init)).astype(BF16)


def _diff_attn_sample(page_table, q, cache_k, cache_v, kn, vn, lq1, lk1, lq2, lk2, onorm, lam_init,
                      dec_seq):
    nb, npages = page_table.shape
    P = PAGES_PER_STEP
    nj = npages // P
    nrow = q.shape[1]
    kern = functools.partial(_diff_sample_kernel, P=P, nj=nj, dec_seq=dec_seq, lam_init=lam_init)
    per_seq = lambda a: pl.BlockSpec((1,) + a.shape[1:], lambda b, j, pt: (b,) + (0,) * (a.ndim - 1))
    vec = pl.BlockSpec((1, DIFF_HD), lambda b, j, pt: (0, 0))
    page = cache_k.shape[1:]
    grid_spec = pltpu.PrefetchScalarGridSpec(
        num_scalar_prefetch=1,
        grid=(nb, nj),
        in_specs=([per_seq(q)] + _page_specs(P, page) + _page_specs(P, page)
                  + [per_seq(kn), per_seq(vn), vec, vec, vec, vec,
                     pl.BlockSpec((1, LANES), lambda b, j, pt: (0, 0))]),
        out_specs=pl.BlockSpec((1, dec_seq, D_MODEL), lambda b, j, pt: (b, 0, 0)),
        scratch_shapes=[pltpu.VMEM((nrow, LANES), BF16), pltpu.VMEM((nrow, PAGE_SIZE * DIFF_HEADS), F32),
                        pltpu.VMEM((nrow, 1), F32), pltpu.VMEM((nrow, 1), F32),
                        pltpu.VMEM((nrow, LANES), F32)],
    )
    return pl.pallas_call(
        kern,
        grid_spec=grid_spec,
        out_shape=jax.ShapeDtypeStruct((nb, dec_seq, D_MODEL), BF16),
        compiler_params=_cparams("arbitrary", "arbitrary"),
        name="diff_attn_sample",
    )(page_table, q, *([cache_k] * P), *([cache_v] * P), kn, vn, lq1, lk1, lq2, lk2, onorm)


def _fox_sample_kernel(pt_ref, q_ref, *rest, P, nj, dec_seq):
    kpages = rest[:P]
    vpages = rest[P:2 * P]
    lpages = rest[2 * P:3 * P]
    kn_ref, vn_ref, ncn_ref, o_ref, qt_sc, run_sc, m_sc, l_sc, acc_sc = rest[3 * P:]
    j = pl.program_id(1)
    nrow = dec_seq * FOX_HEADS

    @pl.when(j == 0)
    def _():
        rq = lax.broadcasted_iota(jnp.int32, (nrow, D_MODEL), 0)
        cq = lax.broadcasted_iota(jnp.int32, (nrow, D_MODEL), 1)
        q = q_ref[0]
        qt_sc[...] = jnp.where(cq // FOX_HD == rq % FOX_HEADS, q, jnp.zeros_like(q))
        run_sc[...] = jnp.zeros_like(run_sc)
        m_sc[...] = jnp.full_like(m_sc, -jnp.inf)
        l_sc[...] = jnp.zeros_like(l_sc)
        acc_sc[...] = jnp.zeros_like(acc_sc)

    r = lax.broadcasted_iota(jnp.int32, (PAGE_SIZE, PAGE_SIZE), 0)
    c = lax.broadcasted_iota(jnp.int32, (PAGE_SIZE, PAGE_SIZE), 1)
    upper = jnp.where(r <= c, 1.0, 0.0).astype(BF16)
    run = run_sc[...]
    cums = []
    for pg in lpages:
        hi, mid, lo = _split3(pg[0])
        local = _dot(hi, upper) + _dot(mid, upper) + _dot(lo, upper)
        cums.append(local + run)
        run = run + local[:, PAGE_SIZE - 1:PAGE_SIZE]
    run_sc[...] = run
    nck = jnp.concatenate(cums, axis=1) * (-LOG2E)

    def flat_t(pages):
        return jnp.concatenate([pg[0].reshape(D_MODEL, PAGE_SIZE).astype(BF16) for pg in pages], axis=1)

    qt = qt_sc[...]
    s = _dot(qt, flat_t(kpages)) + jnp.concatenate([nck] * dec_seq, axis=0)
    alpha, p = _rows_softmax(s, m_sc, l_sc)
    acc = alpha * acc_sc[...] + _dot_nt(p, flat_t(vpages))
    acc_sc[...] = acc

    @pl.when(j == nj - 1)
    def _():
        npad = kn_ref.shape[1]
        ncn = run * (-LOG2E) + ncn_ref[0]
        sn = _dot_nt(qt, kn_ref[0]) + jnp.concatenate([ncn] * dec_seq, axis=0)
        rr = lax.broadcasted_iota(jnp.int32, (nrow, npad), 0)
        cc = lax.broadcasted_iota(jnp.int32, (nrow, npad), 1)
        sn = jnp.where(cc <= rr // FOX_HEADS, sn, -jnp.inf)
        alpha_n, pn = _rows_softmax(sn, m_sc, l_sc)
        o = (alpha_n * acc + _dot(pn, vn_ref[0])) / l_sc[...]
        ro = lax.broadcasted_iota(jnp.int32, (nrow, D_MODEL), 0)
        co = lax.broadcasted_iota(jnp.int32, (nrow, D_MODEL), 1)
        o = jnp.where(co // FOX_HD == ro % FOX_HEADS, o, 0.0)
        rows = [jnp.sum(o[t * FOX_HEADS:(t + 1) * FOX_HEADS], axis=0, keepdims=True)
                for t in range(dec_seq)]
        o_ref[0] = jnp.concatenate(rows, axis=0).astype(BF16)


def _fox_attn_sample(page_table, q, cache_kt, cache_vt, cache_lt, kn, vn, ncn, dec_seq):
    nb, npages = page_table.shape
    P = PAGES_PER_STEP
    nj = npages // P
    nrow = q.shape[1]
    kern = functools.partial(_fox_sample_kernel, P=P, nj=nj, dec_seq=dec_seq)
    per_seq = lambda a: pl.BlockSpec((1,) + a.shape[1:], lambda b, j, pt: (b,) + (0,) * (a.ndim - 1))
    grid_spec = pltpu.PrefetchScalarGridSpec(
        num_scalar_prefetch=1,
        grid=(nb, nj),
        in_specs=([per_seq(q)] + _page_specs(P, cache_kt.shape[1:]) + _page_specs(P, cache_vt.shape[1:])
                  + _page_specs(P, cache_lt.shape[1:]) + [per_seq(kn), per_seq(vn), per_seq(ncn)]),
        out_specs=pl.BlockSpec((1, dec_seq, D_MODEL), lambda b, j, pt: (b, 0, 0)),
        scratch_shapes=[pltpu.VMEM((nrow, D_MODEL), BF16), pltpu.VMEM((FOX_HEADS, 1), F32),
                        pltpu.VMEM((nrow, 1), F32), pltpu.VMEM((nrow, 1), F32),
                        pltpu.VMEM((nrow, D_MODEL), F32)],
    )
    return pl.pallas_call(
        kern,
        grid_spec=grid_spec,
        out_shape=jax.ShapeDtypeStruct((nb, dec_seq, D_MODEL), BF16),
        compiler_params=_cparams("arbitrary", "arbitrary"),
        name="fox_attn_sample",
    )(page_table, q, *([cache_kt] * P), *([cache_vt] * P), *([cache_lt] * P), kn, vn, ncn)


def _sgu_kernel(x_ref, g_ref, win_ref, vn_ref, wmix_ref, bias_ref, wout_ref, *rest,
                tq, shift, emit_v):
    if emit_v:
        y_ref, v_ref, um = rest
    else:
        y_ref, um = rest
    x = x_ref[0]
    hb = _rms_rows(x, g_ref[...]).astype(BF16)

    def gelu(a):
        return 0.5 * a * (1.0 + lax.erf(a * (1.0 / math.sqrt(2.0))))

    v = _rms_rows(gelu(_dot(hb, win_ref[:, SG_WIDTH:2 * SG_WIDTH])), vn_ref[...])
    if emit_v:
        v_ref[0] = v
    vb = v.astype(BF16)
    u = gelu(_dot(hb, win_ref[:, 0:SG_WIDTH]))
    mask = _time_mask(SG_CHUNK, shift)
    for g in range(SG_GROUPS):
        w = jnp.where(mask, wmix_ref[g], 0.0).astype(BF16)
        cs = slice(g * SG_GD, (g + 1) * SG_GD)
        for ch in range(tq // SG_CHUNK):
            rs = slice(ch * SG_CHUNK, (ch + 1) * SG_CHUNK)
            mixed = _dot(w, vb[rs, cs]) + bias_ref[:, g:g + 1]
            um[rs, cs] = (u[rs, cs] * mixed).astype(BF16)
    y_ref[0] = x + _dot(um[...], wout_ref[...])


def _sgu_layer(x, g, win_bf, vnorm, wmix, bias, wout_bf, *, shift, emit_v):
    B, T, D = x.shape
    tq = min(T, SGU_TILE)
    row = pl.BlockSpec((1, tq, D), lambda b, t: (b, t, 0))
    out_specs = [row]
    out_shape = [jax.ShapeDtypeStruct((B, T, D), F32)]
    if emit_v:
        out_specs.append(pl.BlockSpec((1, tq, SG_WIDTH), lambda b, t: (b, t, 0)))
        out_shape.append(jax.ShapeDtypeStruct((B, T, SG_WIDTH), F32))
    kern = functools.partial(_sgu_kernel, tq=tq, shift=shift, emit_v=emit_v)
    return pl.pallas_call(
        kern,
        grid=(B, T // tq),
        in_specs=[row, _const_spec((1, D)), _const_spec(win_bf.shape), _const_spec((1, SG_WIDTH)),
                  _const_spec(wmix.shape), _const_spec(bias.shape), _const_spec(wout_bf.shape)],
        out_specs=out_specs,
        out_shape=out_shape,
        scratch_shapes=[pltpu.VMEM((tq, SG_WIDTH), BF16)],
        compiler_params=_cparams("arbitrary", "arbitrary"),
        name="sgu_mixer",
    )(x, g, win_bf, vnorm, wmix, bias, wout_bf)


def _lambda_init(layer_idx):
    return 0.8 - 0.6 * math.exp(-0.3 * layer_idx)


def _rope_tables(pos):
    half = ROPE_DIM // 2
    inv = ROPE_THETA ** (-jnp.arange(half, dtype=F32) / half)
    ang = pos.astype(F32)[:, None] * inv[None, :]
    lane = jnp.arange(LANES) % SEG
    idx = lane % half
    cos = jnp.where(lane[None, :] < ROPE_DIM, jnp.cos(ang)[:, idx], 1.0)
    sin = jnp.sin(ang)[:, idx]
    s1 = jnp.where(lane[None, :] < half, -sin, 0.0)
    s2 = jnp.where(jnp.logical_and(lane[None, :] >= half, lane[None, :] < ROPE_DIM), sin, 0.0)
    return cos.astype(F32), s1.astype(F32), s2.astype(F32)


def _to_time_major(a):
    a = jnp.swapaxes(a, 0, 1)
    return a.reshape(1, a.shape[0] * a.shape[1], *a.shape[2:])


def _from_time_major(a, nb):
    a = a.reshape(a.shape[1] // nb, nb, *a.shape[2:])
    return jnp.swapaxes(a, 0, 1)


def kernel(x_prompt, x_sample, state_pool, cache_diff_k, cache_diff_v, cache_fox_k, cache_fox_v,
           cache_fox_logf, state_ffn_conv, page_table, norm_mix, norm_ffn, pool_w, pool_scale,
           diff_w_qkv, diff_q_norm, diff_k_norm, diff_lq1, diff_lk1, diff_lq2, diff_lk2, diff_o_norm,
           diff_w_o, fox_w_qkvf, fox_b_f, fox_q_norm, fox_k_norm, fox_w_o, sgu_w_in, sgu_v_norm,
           sgu_w_s, sgu_b_s, sgu_w_out, ffn_w_up, ffn_conv_w, ffn_conv_b, ffn_w_down):
    bp, seq, _ = x_prompt.shape
    nb, dec_seq, _ = x_sample.shape
    n_pool = cache_diff_k.shape[0]
    past_len = page_table.shape[1] * PAGE_SIZE
    row2 = lambda a: a.reshape(1, -1).astype(F32)

    pool_w_bf = pool_w.astype(BF16)
    wu = ffn_w_up.astype(BF16).reshape(DEPTH, D_MODEL, 2 * FFN_NCH, FFN_CK).transpose(0, 2, 1, 3)
    wd = ffn_w_down.astype(BF16)
    cw = ffn_conv_w.reshape(DEPTH, FFN_CONV, 2 * FFN_NCH, FFN_CK).transpose(0, 2, 1, 3)
    cb = ffn_conv_b.reshape(DEPTH, 2 * FFN_NCH, 1, FFN_CK)
    seg_id = jnp.arange(D_MODEL) // SEG
    seg = (seg_id[:, None] == jnp.arange(LANES)[None, :]).astype(BF16)
    segt = jnp.concatenate([seg.T, seg.T], axis=0)
    diff_w_bf = diff_w_qkv.astype(BF16)
    diff_wo_bf = diff_w_o.astype(BF16)
    fox_w_bf = fox_w_qkvf[:, :3 * D_MODEL].astype(BF16)
    fox_wf_bf = jnp.pad(fox_w_qkvf[:, 3 * D_MODEL:], ((0, 0), (0, LANES - FOX_HEADS))).astype(BF16)
    fox_bf = jnp.pad(fox_b_f, (0, LANES - FOX_HEADS)).reshape(1, LANES)
    fox_wo_bf = fox_w_o.astype(BF16)
    tile_seg = lambda g: jnp.tile(g, NSEG).reshape(1, D_MODEL)
    sgu_win_bf = sgu_w_in.astype(BF16)
    sgu_wout_bf = sgu_w_out.astype(BF16)
    lvec = [row2(a) for a in (diff_lq1, diff_lk1, diff_lq2, diff_lk2)]
    onorm = row2(diff_o_norm)

    def trunk(x, *, shift, start, pool_prev, conv_prev, rope, sgu_mix, sgu_bias, attn_diff, attn_fox, emit_v):
        new = {}
        conv_new = []
        for i in range(DEPTH):
            g = row2(norm_mix[i])
            if i == 0:
                x, new["pool"] = _pool_layer(x, pool_prev, g, pool_w_bf, row2(pool_scale),
                                             shift=shift, start=start)
            elif i == 1:
                qb, kf, kb, vf, vb = _diff_proj_layer(x, g, diff_w_bf, seg, segt, tile_seg(diff_q_norm),
                                                      tile_seg(diff_k_norm), *rope)
                new["diff_k"], new["diff_v"] = kf, vf
                x = _oproj_layer(x, attn_diff(qb, kb, vb, _lambda_init(i)), diff_wo_bf)
            elif i == 2:
                qb, kf, kb, vf, vb, lf, nck = _fox_proj_layer(
                    x, g, fox_w_bf, fox_wf_bf, fox_bf, seg, segt, tile_seg(fox_q_norm),
                    tile_seg(fox_k_norm), shift=shift)
                new["fox_k"], new["fox_v"], new["fox_logf"] = kf, vf, lf[..., :FOX_HEADS]
                x = _oproj_layer(x, attn_fox(qb, kb, vb, nck[..., :FOX_HEADS]), fox_wo_bf)
            else:
                res = _sgu_layer(x, g, sgu_win_bf, row2(sgu_v_norm), sgu_mix, sgu_bias, sgu_wout_bf,
                                 shift=shift, emit_v=emit_v)
                x = res[0]
                if emit_v:
                    new["sgu_v"] = res[1]
            x, st = _ffn_layer(x, conv_prev[i], row2(norm_ffn[i]), wu[i], cw[i], cb[i], wd[i], shift=shift)
            conv_new.append(st)
        return x, new, conv_new

    halo_p = SUBLANES
    rope_p = _rope_tables(jnp.arange(seq, dtype=jnp.int32))

    def attn_fox_prompt(qb, kb, vb, nck):
        nck4 = nck.reshape(bp, seq, FOX_HEADS // 2, 2).transpose(0, 2, 1, 3)
        return _fox_attn_prompt(qb, kb, vb, nck4)

    yp, newp, convp = trunk(
        x_prompt, shift=1, start=0,
        pool_prev=jnp.zeros((bp, 2 * SUBLANES, D_MODEL), F32),
        conv_prev=[jnp.zeros((bp, 2 * FFN_NCH, halo_p, FFN_CK), F32)] * DEPTH,
        rope=rope_p,
        sgu_mix=sgu_w_s, sgu_bias=sgu_b_s.T,
        attn_diff=lambda qb, kb, vb, li: _diff_attn_prompt(qb, kb, vb, *lvec, onorm, li),
        attn_fox=attn_fox_prompt, emit_v=False)

    rows = dec_seq * nb
    t_of_row = jnp.arange(rows, dtype=jnp.int32) // nb
    rope_s = _rope_tables(past_len + t_of_row)
    fk_cache = cache_fox_k.transpose(0, 2, 3, 1)
    fv_cache = cache_fox_v.transpose(0, 2, 3, 1)
    fl_cache = cache_fox_logf.transpose(0, 2, 1)
    npad = 2 * SUBLANES

    def seq_major(a):
        return _from_time_major(a, nb)

    def pad_new(a):
        return jnp.pad(seq_major(a), ((0, 0), (0, npad - dec_seq), (0, 0)))

    def attn_diff_sample(qb, kb, vb, li):
        q = seq_major(qb).reshape(nb, dec_seq, DIFF_HEADS, LANES).transpose(0, 2, 1, 3)
        q = jnp.tile(q, (1, 1, 2, 1)).reshape(nb, DIFF_HEADS * 2 * dec_seq, LANES)
        flat_new = lambda a: pad_new(a).reshape(nb, npad * DIFF_HEADS, LANES)
        o = _diff_attn_sample(page_table, q, cache_diff_k, cache_diff_v, flat_new(kb), flat_new(vb),
                              *lvec, onorm, li, dec_seq)
        return _to_time_major(o)

    def attn_fox_sample(qb, kb, vb, nck):
        ncn = jnp.pad(jnp.swapaxes(seq_major(nck), 1, 2), ((0, 0), (0, 0), (0, npad - dec_seq)))
        q = jnp.repeat(seq_major(qb), FOX_HEADS, axis=1)
        o = _fox_attn_sample(page_table, q, fk_cache, fv_cache, fl_cache, pad_new(kb), pad_new(vb), ncn,
                             dec_seq)
        return _to_time_major(o)

    halo_s = (FFN_CONV - 1) * nb
    conv_prev_s = [
        state_ffn_conv[i].reshape(nb, FFN_CONV - 1, 2 * FFN_NCH, FFN_CK).transpose(2, 1, 0, 3)
        .reshape(1, 2 * FFN_NCH, halo_s, FFN_CK) for i in range(DEPTH)]
    sgu_mix_s = jnp.repeat(jnp.repeat(sgu_w_s[:, :dec_seq, :dec_seq], nb, axis=1), nb, axis=2)
    sgu_bias_s = jnp.repeat(sgu_b_s[:, :dec_seq].T, nb, axis=0)
    ys, news, convs = trunk(
        _to_time_major(x_sample), shift=nb, start=past_len,
        pool_prev=_to_time_major(state_pool),
        conv_prev=conv_prev_s, rope=rope_s,
        sgu_mix=sgu_mix_s, sgu_bias=sgu_bias_s,
        attn_diff=attn_diff_sample, attn_fox=attn_fox_sample, emit_v=True)

    def conv_state_prompt(st):
        return st.transpose(0, 2, 1, 3).reshape(bp, FFN_CONV - 1, 2 * FFN_DIM)

    def conv_state_sample(st):
        return (st.reshape(2 * FFN_NCH, FFN_CONV - 1, nb, FFN_CK).transpose(2, 1, 0, 3)
                .reshape(nb, FFN_CONV - 1, 2 * FFN_DIM))

    hd4 = lambda a, h: a.reshape(a.shape[0], a.shape[1], h, D_MODEL // h)
    sm = seq_major
    return (
        yp, sm(ys),
        newp["pool"], sm(news["pool"]),
        hd4(newp["diff_k"], DIFF_HEADS), hd4(newp["diff_v"], DIFF_HEADS),
        hd4(sm(news["diff_k"]), DIFF_HEADS), hd4(sm(news["diff_v"]), DIFF_HEADS),
        hd4(newp["fox_k"], FOX_HEADS), hd4(newp["fox_v"], FOX_HEADS), newp["fox_logf"],
        hd4(sm(news["fox_k"]), FOX_HEADS), hd4(sm(news["fox_v"]), FOX_HEADS), sm(news["fox_logf"]),
        sm(news["sgu_v"]),
        jnp.stack([conv_state_prompt(s) for s in convp], axis=0),
        jnp.stack([conv_state_sample(s) for s in convs], axis=0),
    )
```

```python
import functools
import math

import jax
import jax.numpy as jnp
from jax import lax
from jax.experimental import pallas as pl
from jax.experimental.pallas import tpu as pltpu

F32 = jnp.float32
BF16 = jnp.bfloat16

D_MODEL = 1024
DEPTH = 4
PAGE_SIZE = 128
POOL_WINDOWS = (2, 4, 8, 16)
POOL_GD = D_MODEL // len(POOL_WINDOWS)
POOL_BUF = max(POOL_WINDOWS) - 1
DIFF_HEADS = 8
DIFF_HD = 64
FOX_HEADS = 16
FOX_HD = 64
SG_WIDTH = 2 * D_MODEL
SG_GROUPS = 8
SG_GD = SG_WIDTH // SG_GROUPS
SG_CHUNK = 128
FFN_DIM = 2816
FFN_CONV = 3
ROPE_THETA = 500000.0
ROPE_DIM = DIFF_HD // 4
NORM_EPS = 1e-6

LANES = 128
SUBLANES = 8
FFN_CK = 256
FFN_NCH = FFN_DIM // FFN_CK
SEG = 64
NSEG = D_MODEL // SEG
PAGES_PER_STEP = 8
ROW_TILE = 512
SGU_TILE = 256
ATT_TILE = 512
VMEM_LIMIT = 56 * 1024 * 1024
LOG2E = math.log2(math.e)


def _cparams(*sem):
    return pltpu.CompilerParams(dimension_semantics=sem, vmem_limit_bytes=VMEM_LIMIT)


def _rms_rows(x, g):
    ms = jnp.mean(x * x, axis=-1, keepdims=True)
    return x * lax.rsqrt(ms + NORM_EPS) * g


def _dot(a, b):
    return jnp.dot(a, b, preferred_element_type=F32)


def _dot_nt(a, b):
    return lax.dot_general(a, b, (((1,), (1,)), ((), ())), preferred_element_type=F32)


def _split3(a):
    hi = a.astype(BF16)
    r1 = a - hi.astype(F32)
    mid = r1.astype(BF16)
    lo = (r1 - mid.astype(F32)).astype(BF16)
    return hi, mid, lo


def _time_mask(n, shift):
    r = lax.broadcasted_iota(jnp.int32, (n, n), 0)
    c = lax.broadcasted_iota(jnp.int32, (n, n), 1)
    m = c <= r
    if shift > 1:
        m = jnp.logical_and(m, (r - c) % shift == 0)
    return m


def _const_spec(shape):
    nd = len(shape)
    return pl.BlockSpec(shape, lambda *_: (0,) * nd, pipeline_mode=pl.Buffered(1))


def _pool_kernel(x_ref, prev_ref, g_ref, w_ref, scale_ref, y_ref, st_ref, hext,
                 *, tq, shift, halo, start, nt):
    t = pl.program_id(1)

    @pl.when(t == 0)
    def _():
        hext[0:halo, :] = prev_ref[0]

    x = x_ref[0]
    h = _rms_rows(x, g_ref[...])
    hext[halo:halo + tq, :] = h
    row = lax.broadcasted_iota(jnp.int32, (tq, 1), 0)
    pos = start + (t * tq + row) // shift
    for g, win in enumerate(POOL_WINDOWS):
        c0 = g * POOL_GD
        hg = h[:, c0:c0 + POOL_GD]
        s = hg
        for j in range(1, win):
            o = halo - j * shift
            s = s + hext[o:o + tq, c0:c0 + POOL_GD]
        cnt = jnp.minimum(pos + 1, win).astype(F32)
        d = (s / cnt - hg).astype(BF16)
        y = _dot(d, w_ref[g])
        y_ref[0, :, c0:c0 + POOL_GD] = x[:, c0:c0 + POOL_GD] + y * scale_ref[:, c0:c0 + POOL_GD]

    n = POOL_BUF * shift

    @pl.when(t == nt - 1)
    def _():
        st_ref[0] = hext[halo + tq - n:halo + tq, :]

    if nt > 1:
        hext[0:halo, :] = hext[tq:tq + halo, :]


def _pool_layer(x, prev, g, w_bf, scale, *, shift, start):
    B, T, D = x.shape
    tq = min(T, ROW_TILE)
    nt = T // tq
    halo = prev.shape[1]
    n = POOL_BUF * shift
    kern = functools.partial(_pool_kernel, tq=tq, shift=shift, halo=halo, start=start, nt=nt)
    return pl.pallas_call(
        kern,
        grid=(B, nt),
        in_specs=[
            pl.BlockSpec((1, tq, D), lambda b, t: (b, t, 0)),
            pl.BlockSpec((1, halo, D), lambda b, t: (b, 0, 0)),
            _const_spec((1, D)),
            _const_spec(w_bf.shape),
            _const_spec((1, D)),
        ],
        out_specs=[
            pl.BlockSpec((1, tq, D), lambda b, t: (b, t, 0)),
            pl.BlockSpec((1, n, D), lambda b, t: (b, 0, 0)),
        ],
        out_shape=[jax.ShapeDtypeStruct((B, T, D), F32), jax.ShapeDtypeStruct((B, n, D), F32)],
        scratch_shapes=[pltpu.VMEM((halo + tq, D), F32)],
        compiler_params=_cparams("arbitrary", "arbitrary"),
        name="pool_mixer",
    )(x, prev, g, w_bf, scale)


def _ffn_kernel(x_ref, *rest, tq, shift, halo, nt, with_mixer_out):
    if with_mixer_out:
        o_ref, wo_ref = rest[:2]
        rest = rest[2:]
    prev_ref, g_ref, wu_ref, cw_ref, cb_ref, wd_ref, y_ref, st_ref, hb, buf, act = rest
    t = pl.program_id(1)
    x = x_ref[0]
    if with_mixer_out:
        x = x + _dot(o_ref[0], wo_ref[...])
    hb[...] = _rms_rows(x, g_ref[...]).astype(BF16)

    @pl.when(t == 0)
    def _():
        buf[:, 0:halo, :] = prev_ref[0]

    def conv(idx):
        u = _dot(hb[...], wu_ref[idx])
        buf[idx, halo:halo + tq, :] = u
        w = cw_ref[idx]
        o2 = halo - 2 * shift
        o1 = halo - shift
        out = (cb_ref[idx] + w[0:1] * buf[idx, o2:o2 + tq, :]
               + w[1:2] * buf[idx, o1:o1 + tq, :] + w[2:3] * u)
        buf[idx, 0:halo, :] = buf[idx, tq:tq + halo, :]
        return out

    for c in range(FFN_NCH):
        ag = conv(c)
        av = conv(FFN_NCH + c)
        act[:, c * FFN_CK:(c + 1) * FFN_CK] = (ag / (1.0 + jnp.exp(-ag)) * av).astype(BF16)
    y_ref[0] = x + _dot(act[...], wd_ref[...])

    @pl.when(t == nt - 1)
    def _():
        st_ref[0] = buf[:, halo - 2 * shift:halo, :]


def _ffn_layer(x, prev, g, wu, cw, cb, wd, *, shift, mixer_out=None):
    B, T, D = x.shape
    tq = min(T, ROW_TILE)
    nt = T // tq
    halo = prev.shape[2]
    ns = (FFN_CONV - 1) * shift
    kern = functools.partial(_ffn_kernel, tq=tq, shift=shift, halo=halo, nt=nt,
                             with_mixer_out=mixer_out is not None)
    row = pl.BlockSpec((1, tq, D), lambda b, t: (b, t, 0))
    extra_specs, extra_args = [], []
    if mixer_out is not None:
        extra_specs = [row, _const_spec(mixer_out[1].shape)]
        extra_args = list(mixer_out)
    return pl.pallas_call(
        kern,
        grid=(B, nt),
        in_specs=[row] + extra_specs + [
            pl.BlockSpec((1, 2 * FFN_NCH, halo, FFN_CK), lambda b, t: (b, 0, 0, 0)),
            _const_spec((1, D)),
            _const_spec(wu.shape),
            _const_spec(cw.shape),
            _const_spec(cb.shape),
            _const_spec(wd.shape),
        ],
        out_specs=[
            pl.BlockSpec((1, tq, D), lambda b, t: (b, t, 0)),
            pl.BlockSpec((1, 2 * FFN_NCH, ns, FFN_CK), lambda b, t: (b, 0, 0, 0)),
        ],
        out_shape=[jax.ShapeDtypeStruct((B, T, D), F32),
                   jax.ShapeDtypeStruct((B, 2 * FFN_NCH, ns, FFN_CK), F32)],
        scratch_shapes=[pltpu.VMEM((tq, D), BF16),
                        pltpu.VMEM((2 * FFN_NCH, halo + tq, FFN_CK), F32),
                        pltpu.VMEM((tq, FFN_DIM), BF16)],
        compiler_params=_cparams("arbitrary", "arbitrary"),
        name="conv_ffn",
    )(x, *extra_args, prev, g, wu, cw, cb, wd)


def _seg_norm(a, seg_ref, segt_ref, gain):
    ms = _dot((a * a).astype(BF16), seg_ref[...]) * (1.0 / SEG)
    r = lax.rsqrt(ms + NORM_EPS)
    hi = r.astype(BF16)
    lo = (r - hi.astype(F32)).astype(BF16)
    rf = _dot(jnp.concatenate([hi, lo], axis=1), segt_ref[...])
    return a * rf * gain


def _rope_slab(s, cos, sin_lo, sin_hi):
    return s * cos + pltpu.roll(s, LANES - ROPE_DIM // 2, 1) * sin_lo + pltpu.roll(s, ROPE_DIM // 2, 1) * sin_hi


def _diff_proj_kernel(x_ref, g_ref, w_ref, seg_ref, segt_ref, qg_ref, kg_ref, cos_ref, s1_ref, s2_ref,
                      qb_ref, kf_ref, kb_ref, vf_ref, vb_ref, *, scale):
    hb = _rms_rows(x_ref[0], g_ref[...]).astype(BF16)
    cos, s1, s2 = cos_ref[...], s1_ref[...], s2_ref[...]
    q = _seg_norm(_dot(hb, w_ref[:, 0:D_MODEL]), seg_ref, segt_ref, qg_ref[...])
    for c in range(D_MODEL // LANES):
        sl = slice(c * LANES, (c + 1) * LANES)
        qb_ref[0, :, sl] = (_rope_slab(q[:, sl], cos, s1, s2) * scale).astype(BF16)
    k = _seg_norm(_dot(hb, w_ref[:, D_MODEL:2 * D_MODEL]), seg_ref, segt_ref, kg_ref[...])
    for c in range(D_MODEL // LANES):
        sl = slice(c * LANES, (c + 1) * LANES)
        kr = _rope_slab(k[:, sl], cos, s1, s2)
        kf_ref[0, :, sl] = kr
        kb_ref[0, :, sl] = kr.astype(BF16)
    v = _dot(hb, w_ref[:, 2 * D_MODEL:3 * D_MODEL])
    vf_ref[0] = v
    vb_ref[0] = v.astype(BF16)


def _diff_proj_layer(x, g, w_bf, seg, segt, qg, kg, cos, s1, s2):
    B, T, D = x.shape
    tq = min(T, ROW_TILE)
    row = pl.BlockSpec((1, tq, D), lambda b, t: (b, t, 0))
    tab = pl.BlockSpec((tq, LANES), lambda b, t: (t, 0))
    kern = functools.partial(_diff_proj_kernel, scale=DIFF_HD ** -0.5 * LOG2E)
    return pl.pallas_call(
        kern,
        grid=(B, T // tq),
        in_specs=[row, _const_spec((1, D)), _const_spec(w_bf.shape), _const_spec(seg.shape),
                  _const_spec(segt.shape), _const_spec((1, D)), _const_spec((1, D)), tab, tab, tab],
        out_specs=[row, row, row, row, row],
        out_shape=[jax.ShapeDtypeStruct((B, T, D), BF16), jax.ShapeDtypeStruct((B, T, D), F32),
                   jax.ShapeDtypeStruct((B, T, D), BF16), jax.ShapeDtypeStruct((B, T, D), F32),
                   jax.ShapeDtypeStruct((B, T, D), BF16)],
        compiler_params=_cparams("arbitrary", "arbitrary"),
        name="diff_qkv_proj",
    )(x, g, w_bf, seg, segt, qg, kg, cos, s1, s2)


def _fox_proj_kernel(x_ref, g_ref, w_ref, wf_ref, bf_ref, seg_ref, segt_ref, qg_ref, kg_ref,
                     qb_ref, kf_ref, kb_ref, vf_ref, vb_ref, lf_ref, nck_ref, run,
                     *, tq, shift, scale):
    t = pl.program_id(1)

    @pl.when(t == 0)
    def _():
        run[...] = jnp.zeros_like(run)

    hb = _rms_rows(x_ref[0], g_ref[...]).astype(BF16)
    q = _seg_norm(_dot(hb, w_ref[:, 0:D_MODEL]), seg_ref, segt_ref, qg_ref[...])
    qb_ref[0] = (q * scale).astype(BF16)
    k = _seg_norm(_dot(hb, w_ref[:, D_MODEL:2 * D_MODEL]), seg_ref, segt_ref, kg_ref[...])
    kf_ref[0] = k
    kb_ref[0] = k.astype(BF16)
    v = _dot(hb, w_ref[:, 2 * D_MODEL:3 * D_MODEL])
    vf_ref[0] = v
    vb_ref[0] = v.astype(BF16)
    z = _dot(hb, wf_ref[...]) + bf_ref[...]
    lf = jnp.minimum(z, 0.0) - jnp.log1p(jnp.exp(-jnp.abs(z)))
    lf_ref[0] = lf
    m = jnp.where(_time_mask(tq, shift), 1.0, 0.0).astype(BF16)
    hi, mid, lo = _split3(lf)
    cum = _dot(m, hi) + _dot(m, mid) + _dot(m, lo) + run[...]
    nck_ref[0] = cum * (-LOG2E)
    run[...] = cum[tq - 1:tq, :]


def _fox_proj_layer(x, g, w_bf, wf_bf, bf, seg, segt, qg, kg, *, shift):
    B, T, D = x.shape
    tq = min(T, ROW_TILE)
    row = pl.BlockSpec((1, tq, D), lambda b, t: (b, t, 0))
    nar = pl.BlockSpec((1, tq, LANES), lambda b, t: (b, t, 0))
    kern = functools.partial(_fox_proj_kernel, tq=tq, shift=shift, scale=FOX_HD ** -0.5 * LOG2E)
    return pl.pallas_call(
        kern,
        grid=(B, T // tq),
        in_specs=[row, _const_spec((1, D)), _const_spec(w_bf.shape), _const_spec(wf_bf.shape),
                  _const_spec((1, LANES)), _const_spec(seg.shape), _const_spec(segt.shape),
                  _const_spec((1, D)), _const_spec((1, D))],
        out_specs=[row, row, row, row, row, nar, nar],
        out_shape=[jax.ShapeDtypeStruct((B, T, D), BF16), jax.ShapeDtypeStruct((B, T, D), F32),
                   jax.ShapeDtypeStruct((B, T, D), BF16), jax.ShapeDtypeStruct((B, T, D), F32),
                   jax.ShapeDtypeStruct((B, T, D), BF16), jax.ShapeDtypeStruct((B, T, LANES), F32),
                   jax.ShapeDtypeStruct((B, T, LANES), F32)],
        scratch_shapes=[pltpu.VMEM((1, LANES), F32)],
        compiler_params=_cparams("arbitrary", "arbitrary"),
        name="fox_qkvf_proj",
    )(x, g, w_bf, wf_bf, bf, seg, segt, qg, kg)


def _diff_lambda(lq1, lk1, lq2, lk2, lam_init):
    a = jnp.sum(lq1[...] * lk1[...], axis=-1, keepdims=True)
    b = jnp.sum(lq2[...] * lk2[...], axis=-1, keepdims=True)
    return jnp.exp(a) - jnp.exp(b) + lam_init


ATT_CHUNK = 32


def _softmax_tile_t(s_ref, p_ref, vt, m, l, acc, masked):
    tk, tq = s_ref.shape
    nch = tk // ATT_CHUNK

    def chunk(c):
        blk = s_ref[c * ATT_CHUNK:(c + 1) * ATT_CHUNK, :]
        if masked:
            key = lax.broadcasted_iota(jnp.int32, (ATT_CHUNK, tq), 0) + c * ATT_CHUNK
            qry = lax.broadcasted_iota(jnp.int32, (ATT_CHUNK, tq), 1)
            blk = jnp.where(key <= qry, blk, -jnp.inf)
        return blk

    def fold(x, op):
        return op(x.reshape(ATT_CHUNK // SUBLANES, SUBLANES, tq), axis=0)

    mx = fold(chunk(0), jnp.max)
    for c in range(1, nch):
        mx = jnp.maximum(mx, fold(chunk(c), jnp.max))
    m_new = jnp.maximum(m, jnp.max(mx, axis=0, keepdims=True))
    alpha = jnp.exp2(m - m_new)
    ls = None
    for c in range(nch):
        p = jnp.exp2(chunk(c) - m_new)
        ps = fold(p, jnp.sum)
        ls = ps if ls is None else ls + ps
        p_ref[c * ATT_CHUNK:(c + 1) * ATT_CHUNK, :] = p.astype(BF16)
    l = alpha * l + jnp.sum(ls, axis=0, keepdims=True)
    acc = alpha * acc + _dot(vt, p_ref[...])
    return m_new, l, acc


def _stage_vt(v_ref, vt_sc, tq):
    for j in range(vt_sc.shape[0]):
        vt_sc[j] = v_ref[0, j * tq:(j + 1) * tq, :].astype(F32).T.astype(BF16)


def _flash_tiles(qi, put, consume, init, slot_a, slot_b):
    put(0, slot_a)

    def pair(a, stats):
        put(2 * a + 1, slot_b)
        stats = consume(2 * a, slot_a, stats, False)
        put(2 * a + 2, slot_a)
        return consume(2 * a + 1, slot_b, stats, False)

    stats = lax.fori_loop(0, qi // 2, pair, init)

    def last_in_a():
        return consume(qi, slot_a, stats, True)

    def last_in_b():
        put(qi, slot_b)
        return consume(qi, slot_b, consume(qi - 1, slot_a, stats, False), True)

    return lax.cond(qi % 2 == 0, last_in_a, last_in_b)


def _diff_attn_kernel(q_ref, k_ref, v_ref, lq1, lk1, lq2, lk2, on_ref, o_ref, vt_sc, sa_sc, sb_sc, p_sc,
                      *, tq, lam_init):
    qi = pl.program_id(2)

    @pl.when(qi == 0)
    def _():
        _stage_vt(v_ref, vt_sc, tq)

    q = q_ref[0]
    lo = lax.broadcasted_iota(jnp.int32, (1, LANES), 1) < DIFF_HD
    zero = jnp.zeros_like(q)
    qs = (jnp.where(lo, q, zero), jnp.where(lo, zero, q))

    def put(kj, slot):
        k = k_ref[0, pl.ds(pl.multiple_of(kj * tq, tq), tq), :]
        for i in range(2):
            slot[i] = _dot_nt(k, qs[i])

    def consume(kj, slot, stats, masked):
        vt = vt_sc[kj]
        out = []
        for i in range(2):
            out.extend(_softmax_tile_t(slot.at[i], p_sc.at[i], vt, *stats[3 * i:3 * i + 3], masked))
        return tuple(out)

    init = (jnp.full((1, tq), -jnp.inf, F32), jnp.zeros((1, tq), F32), jnp.zeros((LANES, tq), F32)) * 2
    m1, l1, a1, m2, l2, a2 = _flash_tiles(qi, put, consume, init, sa_sc, sb_sc)
    lam = _diff_lambda(lq1, lk1, lq2, lk2, lam_init)
    o = (a1 / l1 - lam * (a2 / l2)).T
    o_ref[0] = (_rms_rows(o, on_ref[...]) * (1.0 - lam_init)).astype(BF16)


def _attn_scratch(T, tq):
    return [pltpu.VMEM((T // tq, LANES, tq), BF16), pltpu.VMEM((2, tq, tq), F32),
            pltpu.VMEM((2, tq, tq), F32), pltpu.VMEM((2, tq, tq), BF16)]


def _diff_attn_prompt(qb, kb, vb, lq1, lk1, lq2, lk2, onorm, lam_init):
    B, T, D = qb.shape
    tq = ATT_TILE
    kern = functools.partial(_diff_attn_kernel, tq=tq, lam_init=lam_init)
    vec = _const_spec((1, DIFF_HD))
    return pl.pallas_call(
        kern,
        grid=(B, DIFF_HEADS, T // tq),
        in_specs=[
            pl.BlockSpec((1, tq, LANES), lambda b, h, i: (b, i, h)),
            pl.BlockSpec((1, T, LANES), lambda b, h, i: (b, 0, h)),
            pl.BlockSpec((1, T, LANES), lambda b, h, i: (b, 0, h)),
            vec, vec, vec, vec, _const_spec((1, LANES)),
        ],
        out_specs=pl.BlockSpec((1, tq, LANES), lambda b, h, i: (b, i, h)),
        out_shape=jax.ShapeDtypeStruct((B, T, D), BF16),
        scratch_shapes=_attn_scratch(T, tq),
        compiler_params=_cparams("arbitrary", "arbitrary", "arbitrary"),
        name="diff_attn_prompt",
    )(qb, kb, vb, lq1, lk1, lq2, lk2, onorm)


def _fox_attn_kernel(q_ref, k_ref, v_ref, nck_ref, o_ref, vt_sc, sa_sc, sb_sc, p_sc, nck_sc, *, tq):
    qi = pl.program_id(2)

    @pl.when(qi == 0)
    def _():
        _stage_vt(v_ref, vt_sc, tq)
        lane = lax.broadcasted_iota(jnp.int32, (1, LANES), 1)
        for i in range(2):
            own = lane == 2 * pl.program_id(1) + i
            for j in range(vt_sc.shape[0]):
                rows = slice(j * tq, (j + 1) * tq)
                col = jnp.sum(jnp.where(own, nck_ref[0, rows, :], 0.0), axis=-1, keepdims=True)
                nck_sc[i, rows, :] = jnp.broadcast_to(col, (tq, LANES))

    q = q_ref[0]
    lo = lax.broadcasted_iota(jnp.int32, (1, LANES), 1) < FOX_HD
    zero = jnp.zeros_like(q)
    qs = (jnp.where(lo, q, zero), jnp.where(lo, zero, q))

    def put(kj, slot):
        off = pl.multiple_of(kj * tq, tq)
        k = k_ref[0, pl.ds(off, tq), :]
        for i in range(2):
            nck = nck_sc[i, pl.ds(off, tq), :]
            slot[i] = _dot_nt(k, qs[i]) + jnp.concatenate([nck] * (tq // LANES), axis=1)

    def consume(kj, slot, stats, masked):
        vt = vt_sc[kj]
        out = []
        for i in range(2):
            out.extend(_softmax_tile_t(slot.at[i], p_sc.at[i], vt[i * FOX_HD:(i + 1) * FOX_HD],
                                       *stats[3 * i:3 * i + 3], masked))
        return tuple(out)

    init = (jnp.full((1, tq), -jnp.inf, F32), jnp.zeros((1, tq), F32), jnp.zeros((FOX_HD, tq), F32)) * 2
    m1, l1, a1, m2, l2, a2 = _flash_tiles(qi, put, consume, init, sa_sc, sb_sc)
    o_ref[0] = jnp.concatenate([a1 / l1, a2 / l2], axis=0).T.astype(BF16)


def _fox_attn_prompt(qb, kb, vb, nck):
    B, T, D = qb.shape
    tq = ATT_TILE
    nh = FOX_HEADS // 2
    kern = functools.partial(_fox_attn_kernel, tq=tq)
    return pl.pallas_call(
        kern,
        grid=(B, nh, T // tq),
        in_specs=[
            pl.BlockSpec((1, tq, LANES), lambda b, h, i: (b, i, h)),
            pl.BlockSpec((1, T, LANES), lambda b, h, i: (b, 0, h)),
            pl.BlockSpec((1, T, LANES), lambda b, h, i: (b, 0, h)),
            pl.BlockSpec((1, T, LANES), lambda b, h, i: (b, 0, 0)),
        ],
        out_specs=pl.BlockSpec((1, tq, LANES), lambda b, h, i: (b, i, h)),
        out_shape=jax.ShapeDtypeStruct((B, T, D), BF16),
        scratch_shapes=_attn_scratch(T, tq) + [pltpu.VMEM((2, T, LANES), F32)],
        compiler_params=_cparams("arbitrary", "arbitrary", "arbitrary"),
        name="fox_attn_prompt",
    )(qb, kb, vb, nck)


def _page_specs(n_pages_step, page_shape):
    P = n_pages_step
    zeros = (0,) * len(page_shape)

    def mk(p):
        return pl.BlockSpec((1,) + tuple(page_shape), lambda b, j, pt: (pt[b, j * P + p],) + zeros)

    return [mk(p) for p in range(P)]


def _rows_softmax(s, m_sc, l_sc):
    m = m_sc[...]
    m_new = jnp.maximum(m, jnp.max(s, axis=-1, keepdims=True))
    alpha = jnp.exp2(m - m_new)
    p = jnp.exp2(s - m_new)
    l_sc[...] = alpha * l_sc[...] + jnp.sum(p, axis=-1, keepdims=True)
    m_sc[...] = m_new
    return alpha, p.astype(BF16)


def _diff_sample_kernel(pt_ref, q_ref, *rest, P, nj, dec_seq, lam_init):
    kpages = rest[:P]
    vpages = rest[P:2 * P]
    kn_ref, vn_ref, lq1, lk1, lq2, lk2, on_ref, o_ref, qm_sc, mask_sc, m_sc, l_sc, acc_sc = rest[2 * P:]
    j = pl.program_id(1)
    nr = 2 * dec_seq
    nrow = DIFF_HEADS * nr
    ncol = PAGE_SIZE * DIFF_HEADS

    @pl.when(j == 0)
    def _():
        r = lax.broadcasted_iota(jnp.int32, (nrow, LANES), 0)
        c = lax.broadcasted_iota(jnp.int32, (nrow, LANES), 1)
        q = q_ref[0]
        qm_sc[...] = jnp.where(c // DIFF_HD == (r % nr) // dec_seq, q, jnp.zeros_like(q))
        rr = lax.broadcasted_iota(jnp.int32, (nrow, ncol), 0)
        cc = lax.broadcasted_iota(jnp.int32, (nrow, ncol), 1)
        mask_sc[...] = jnp.where(cc % DIFF_HEADS == rr // nr, 0.0, -jnp.inf)
        m_sc[...] = jnp.full_like(m_sc, -jnp.inf)
        l_sc[...] = jnp.zeros_like(l_sc)
        acc_sc[...] = jnp.zeros_like(acc_sc)

    def flat(pg):
        return pg[0].reshape(ncol, LANES).astype(BF16)

    qm = qm_sc[...]
    ss = [_dot_nt(qm, flat(pg)) + mask_sc[...] for pg in kpages]
    m = m_sc[...]
    m_new = m
    for s in ss:
        m_new = jnp.maximum(m_new, jnp.max(s, axis=-1, keepdims=True))
    alpha = jnp.exp2(m - m_new)
    l = alpha * l_sc[...]
    acc = alpha * acc_sc[...]
    for s, pg in zip(ss, vpages):
        p = jnp.exp2(s - m_new)
        l = l + jnp.sum(p, axis=-1, keepdims=True)
        acc = acc + _dot(p.astype(BF16), flat(pg))
    m_sc[...] = m_new
    l_sc[...] = l
    acc_sc[...] = acc

    @pl.when(j == nj - 1)
    def _():
        npad = kn_ref.shape[1]
        rr = lax.broadcasted_iota(jnp.int32, (nrow, npad), 0)
        cc = lax.broadcasted_iota(jnp.int32, (nrow, npad), 1)
        allowed = jnp.logical_and(cc % DIFF_HEADS == rr // nr, cc // DIFF_HEADS <= rr % dec_seq)
        sn = jnp.where(allowed, _dot_nt(qm, kn_ref[0]), -jnp.inf)
        alpha_n, pn = _rows_softmax(sn, m_sc, l_sc)
        o = (alpha_n * acc + _dot(pn, vn_ref[0])) / l_sc[...]
        lam = _diff_lambda(lq1, lk1, lq2, lk2, lam_init)
        for h in range(DIFF_HEADS):
            od = o[h * nr:h * nr + dec_seq] - lam * o[h * nr + dec_seq:(h + 1) * nr]
            o_ref[0, :, h * LANES:(h + 1) * LANES] = (
                _rms_rows(od, on_ref[...]) * (1.0 - lam_init)).astype(BF16)


def _diff_attn_sample(page_table, q, cache_k, cache_v, kn, vn, lq1, lk1, lq2, lk2, onorm, lam_init,
                      dec_seq):
    nb, npages = page_table.shape
    P = PAGES_PER_STEP
    nj = npages // P
    nrow = q.shape[1]
    kern = functools.partial(_diff_sample_kernel, P=P, nj=nj, dec_seq=dec_seq, lam_init=lam_init)
    per_seq = lambda a: pl.BlockSpec((1,) + a.shape[1:], lambda b, j, pt: (b,) + (0,) * (a.ndim - 1))
    vec = pl.BlockSpec((1, DIFF_HD), lambda b, j, pt: (0, 0))
    page = cache_k.shape[1:]
    grid_spec = pltpu.PrefetchScalarGridSpec(
        num_scalar_prefetch=1,
        grid=(nb, nj),
        in_specs=([per_seq(q)] + _page_specs(P, page) + _page_specs(P, page)
                  + [per_seq(kn), per_seq(vn), vec, vec, vec, vec,
                     pl.BlockSpec((1, LANES), lambda b, j, pt: (0, 0))]),
        out_specs=pl.BlockSpec((1, dec_seq, D_MODEL), lambda b, j, pt: (b, 0, 0)),
        scratch_shapes=[pltpu.VMEM((nrow, LANES), BF16), pltpu.VMEM((nrow, PAGE_SIZE * DIFF_HEADS), F32),
                        pltpu.VMEM((nrow, 1), F32), pltpu.VMEM((nrow, 1), F32),
                        pltpu.VMEM((nrow, LANES), F32)],
    )
    return pl.pallas_call(
        kern,
        grid_spec=grid_spec,
        out_shape=jax.ShapeDtypeStruct((nb, dec_seq, D_MODEL), BF16),
        compiler_params=_cparams("arbitrary", "arbitrary"),
        name="diff_attn_sample",
    )(page_table, q, *([cache_k] * P), *([cache_v] * P), kn, vn, lq1, lk1, lq2, lk2, onorm)


def _fox_sample_kernel(pt_ref, q_ref, *rest, P, nj, dec_seq):
    kpages = rest[:P]
    vpages = rest[P:2 * P]
    lpages = rest[2 * P:3 * P]
    kn_ref, vn_ref, ncn_ref, o_ref, qt_sc, run_sc, m_sc, l_sc, acc_sc = rest[3 * P:]
    j = pl.program_id(1)
    nrow = dec_seq * FOX_HEADS

    @pl.when(j == 0)
    def _():
        rq = lax.broadcasted_iota(jnp.int32, (nrow, D_MODEL), 0)
        cq = lax.broadcasted_iota(jnp.int32, (nrow, D_MODEL), 1)
        q = q_ref[0]
        qt_sc[...] = jnp.where(cq // FOX_HD == rq % FOX_HEADS, q, jnp.zeros_like(q))
        run_sc[...] = jnp.zeros_like(run_sc)
        m_sc[...] = jnp.full_like(m_sc, -jnp.inf)
        l_sc[...] = jnp.zeros_like(l_sc)
        acc_sc[...] = jnp.zeros_like(acc_sc)

    r = lax.broadcasted_iota(jnp.int32, (PAGE_SIZE, PAGE_SIZE), 0)
    c = lax.broadcasted_iota(jnp.int32, (PAGE_SIZE, PAGE_SIZE), 1)
    upper = jnp.where(r <= c, 1.0, 0.0).astype(BF16)
    run = run_sc[...]
    cums = []
    for pg in lpages:
        hi, mid, lo = _split3(pg[0])
        local = _dot(hi, upper) + _dot(mid, upper) + _dot(lo, upper)
        cums.append(local + run)
        run = run + local[:, PAGE_SIZE - 1:PAGE_SIZE]
    run_sc[...] = run
    nck = jnp.concatenate(cums, axis=1) * (-LOG2E)

    def flat_t(pages):
        return jnp.concatenate([pg[0].reshape(D_MODEL, PAGE_SIZE).astype(BF16) for pg in pages], axis=1)

    qt = qt_sc[...]
    s = _dot(qt, flat_t(kpages)) + jnp.concatenate([nck] * dec_seq, axis=0)
    alpha, p = _rows_softmax(s, m_sc, l_sc)
    acc = alpha * acc_sc[...] + _dot_nt(p, flat_t(vpages))
    acc_sc[...] = acc

    @pl.when(j == nj - 1)
    def _():
        npad = kn_ref.shape[1]
        ncn = run * (-LOG2E) + ncn_ref[0]
        sn = _dot_nt(qt, kn_ref[0]) + jnp.concatenate([ncn] * dec_seq, axis=0)
        rr = lax.broadcasted_iota(jnp.int32, (nrow, npad), 0)
        cc = lax.broadcasted_iota(jnp.int32, (nrow, npad), 1)
        sn = jnp.where(cc <= rr // FOX_HEADS, sn, -jnp.inf)
        alpha_n, pn = _rows_softmax(sn, m_sc, l_sc)
        o = (alpha_n * acc + _dot(pn, vn_ref[0])) / l_sc[...]
        ro = lax.broadcasted_iota(jnp.int32, (nrow, D_MODEL), 0)
        co = lax.broadcasted_iota(jnp.int32, (nrow, D_MODEL), 1)
        o = jnp.where(co // FOX_HD == ro % FOX_HEADS, o, 0.0)
        rows = [jnp.sum(o[t * FOX_HEADS:(t + 1) * FOX_HEADS], axis=0, keepdims=True)
                for t in range(dec_seq)]
        o_ref[0] = jnp.concatenate(rows, axis=0).astype(BF16)


def _fox_attn_sample(page_table, q, cache_kt, cache_vt, cache_lt, kn, vn, ncn, dec_seq):
    nb, npages = page_table.shape
    P = PAGES_PER_STEP
    nj = npages // P
    nrow = q.shape[1]
    kern = functools.partial(_fox_sample_kernel, P=P, nj=nj, dec_seq=dec_seq)
    per_seq = lambda a: pl.BlockSpec((1,) + a.shape[1:], lambda b, j, pt: (b,) + (0,) * (a.ndim - 1))
    grid_spec = pltpu.PrefetchScalarGridSpec(
        num_scalar_prefetch=1,
        grid=(nb, nj),
        in_specs=([per_seq(q)] + _page_specs(P, cache_kt.shape[1:]) + _page_specs(P, cache_vt.shape[1:])
                  + _page_specs(P, cache_lt.shape[1:]) + [per_seq(kn), per_seq(vn), per_seq(ncn)]),
        out_specs=pl.BlockSpec((1, dec_seq, D_MODEL), lambda b, j, pt: (b, 0, 0)),
        scratch_shapes=[pltpu.VMEM((nrow, D_MODEL), BF16), pltpu.VMEM((FOX_HEADS, 1), F32),
                        pltpu.VMEM((nrow, 1), F32), pltpu.VMEM((nrow, 1), F32),
                        pltpu.VMEM((nrow, D_MODEL), F32)],
    )
    return pl.pallas_call(
        kern,
        grid_spec=grid_spec,
        out_shape=jax.ShapeDtypeStruct((nb, dec_seq, D_MODEL), BF16),
        compiler_params=_cparams("arbitrary", "arbitrary"),
        name="fox_attn_sample",
    )(page_table, q, *([cache_kt] * P), *([cache_vt] * P), *([cache_lt] * P), kn, vn, ncn)


def _sgu_kernel(x_ref, g_ref, win_ref, vn_ref, wmix_ref, bias_ref, wout_ref, *rest,
                tq, shift, emit_v):
    if emit_v:
        y_ref, v_ref, um = rest
    else:
        y_ref, um = rest
    x = x_ref[0]
    hb = _rms_rows(x, g_ref[...]).astype(BF16)

    def gelu(a):
        return 0.5 * a * (1.0 + lax.erf(a * (1.0 / math.sqrt(2.0))))

    v = _rms_rows(gelu(_dot(hb, win_ref[:, SG_WIDTH:2 * SG_WIDTH])), vn_ref[...])
    if emit_v:
        v_ref[0] = v
    vb = v.astype(BF16)
    u = gelu(_dot(hb, win_ref[:, 0:SG_WIDTH]))
    mask = _time_mask(SG_CHUNK, shift)
    for g in range(SG_GROUPS):
        w = jnp.where(mask, wmix_ref[g], 0.0).astype(BF16)
        cs = slice(g * SG_GD, (g + 1) * SG_GD)
        for ch in range(tq // SG_CHUNK):
            rs = slice(ch * SG_CHUNK, (ch + 1) * SG_CHUNK)
            mixed = _dot(w, vb[rs, cs]) + bias_ref[:, g:g + 1]
            um[rs, cs] = (u[rs, cs] * mixed).astype(BF16)
    y_ref[0] = x + _dot(um[...], wout_ref[...])


def _sgu_layer(x, g, win_bf, vnorm, wmix, bias, wout_bf, *, shift, emit_v):
    B, T, D = x.shape
    tq = min(T, SGU_TILE)
    row = pl.BlockSpec((1, tq, D), lambda b, t: (b, t, 0))
    out_specs = [row]
    out_shape = [jax.ShapeDtypeStruct((B, T, D), F32)]
    if emit_v:
        out_specs.append(pl.BlockSpec((1, tq, SG_WIDTH), lambda b, t: (b, t, 0)))
        out_shape.append(jax.ShapeDtypeStruct((B, T, SG_WIDTH), F32))
    kern = functools.partial(_sgu_kernel, tq=tq, shift=shift, emit_v=emit_v)
    return pl.pallas_call(
        kern,
        grid=(B, T // tq),
        in_specs=[row, _const_spec((1, D)), _const_spec(win_bf.shape), _const_spec((1, SG_WIDTH)),
                  _const_spec(wmix.shape), _const_spec(bias.shape), _const_spec(wout_bf.shape)],
        out_specs=out_specs,
        out_shape=out_shape,
        scratch_shapes=[pltpu.VMEM((tq, SG_WIDTH), BF16)],
        compiler_params=_cparams("arbitrary", "arbitrary"),
        name="sgu_mixer",
    )(x, g, win_bf, vnorm, wmix, bias, wout_bf)


def _lambda_init(layer_idx):
    return 0.8 - 0.6 * math.exp(-0.3 * layer_idx)


def _rope_tables(pos):
    half = ROPE_DIM // 2
    inv = ROPE_THETA ** (-jnp.arange(half, dtype=F32) / half)
    ang = pos.astype(F32)[:, None] * inv[None, :]
    lane = jnp.arange(LANES) % SEG
    idx = lane % half
    cos = jnp.where(lane[None, :] < ROPE_DIM, jnp.cos(ang)[:, idx], 1.0)
    sin = jnp.sin(ang)[:, idx]
    s1 = jnp.where(lane[None, :] < half, -sin, 0.0)
    s2 = jnp.where(jnp.logical_and(lane[None, :] >= half, lane[None, :] < ROPE_DIM), sin, 0.0)
    return cos.astype(F32), s1.astype(F32), s2.astype(F32)


def _to_time_major(a):
    a = jnp.swapaxes(a, 0, 1)
    return a.reshape(1, a.shape[0] * a.shape[1], *a.shape[2:])


def _from_time_major(a, nb):
    a = a.reshape(a.shape[1] // nb, nb, *a.shape[2:])
    return jnp.swapaxes(a, 0, 1)


def kernel(x_prompt, x_sample, state_pool, cache_diff_k, cache_diff_v, cache_fox_k, cache_fox_v,
           cache_fox_logf, state_ffn_conv, page_table, norm_mix, norm_ffn, pool_w, pool_scale,
           diff_w_qkv, diff_q_norm, diff_k_norm, diff_lq1, diff_lk1, diff_lq2, diff_lk2, diff_o_norm,
           diff_w_o, fox_w_qkvf, fox_b_f, fox_q_norm, fox_k_norm, fox_w_o, sgu_w_in, sgu_v_norm,
           sgu_w_s, sgu_b_s, sgu_w_out, ffn_w_up, ffn_conv_w, ffn_conv_b, ffn_w_down):
    bp, seq, _ = x_prompt.shape
    nb, dec_seq, _ = x_sample.shape
    n_pool = cache_diff_k.shape[0]
    past_len = page_table.shape[1] * PAGE_SIZE
    row2 = lambda a: a.reshape(1, -1).astype(F32)

    pool_w_bf = pool_w.astype(BF16)
    wu = ffn_w_up.astype(BF16).reshape(DEPTH, D_MODEL, 2 * FFN_NCH, FFN_CK).transpose(0, 2, 1, 3)
    wd = ffn_w_down.astype(BF16)
    cw = ffn_conv_w.reshape(DEPTH, FFN_CONV, 2 * FFN_NCH, FFN_CK).transpose(0, 2, 1, 3)
    cb = ffn_conv_b.reshape(DEPTH, 2 * FFN_NCH, 1, FFN_CK)
    seg_id = jnp.arange(D_MODEL) // SEG
    seg = (seg_id[:, None] == jnp.arange(LANES)[None, :]).astype(BF16)
    segt = jnp.concatenate([seg.T, seg.T], axis=0)
    diff_w_bf = diff_w_qkv.astype(BF16)
    diff_wo_bf = diff_w_o.astype(BF16)
    fox_w_bf = fox_w_qkvf[:, :3 * D_MODEL].astype(BF16)
    fox_wf_bf = jnp.pad(fox_w_qkvf[:, 3 * D_MODEL:], ((0, 0), (0, LANES - FOX_HEADS))).astype(BF16)
    fox_bf = jnp.pad(fox_b_f, (0, LANES - FOX_HEADS)).reshape(1, LANES)
    fox_wo_bf = fox_w_o.astype(BF16)
    tile_seg = lambda g: jnp.tile(g, NSEG).reshape(1, D_MODEL)
    sgu_win_bf = sgu_w_in.astype(BF16)
    sgu_wout_bf = sgu_w_out.astype(BF16)
    lvec = [row2(a) for a in (diff_lq1, diff_lk1, diff_lq2, diff_lk2)]
    onorm = row2(diff_o_norm)

    def trunk(x, *, shift, start, pool_prev, conv_prev, rope, sgu_mix, sgu_bias, attn_diff, attn_fox, emit_v):
        new = {}
        conv_new = []
        for i in range(DEPTH):
            g = row2(norm_mix[i])
            mixer_out = None
            if i == 0:
                x, new["pool"] = _pool_layer(x, pool_prev, g, pool_w_bf, row2(pool_scale),
                                             shift=shift, start=start)
            elif i == 1:
                qb, kf, kb, vf, vb = _diff_proj_layer(x, g, diff_w_bf, seg, segt, tile_seg(diff_q_norm),
                                                      tile_seg(diff_k_norm), *rope)
                new["diff_k"], new["diff_v"] = kf, vf
                mixer_out = (attn_diff(qb, kb, vb, _lambda_init(i)), diff_wo_bf)
            elif i == 2:
                qb, kf, kb, vf, vb, lf, nck = _fox_proj_layer(
                    x, g, fox_w_bf, fox_wf_bf, fox_bf, seg, segt, tile_seg(fox_q_norm),
                    tile_seg(fox_k_norm), shift=shift)
                new["fox_k"], new["fox_v"], new["fox_logf"] = kf, vf, lf[..., :FOX_HEADS]
                mixer_out = (attn_fox(qb, kb, vb, nck), fox_wo_bf)
            else:
                res = _sgu_layer(x, g, sgu_win_bf, row2(sgu_v_norm), sgu_mix, sgu_bias, sgu_wout_bf,
                                 shift=shift, emit_v=emit_v)
                x = res[0]
                if emit_v:
                    new["sgu_v"] = res[1]
            x, st = _ffn_layer(x, conv_prev[i], row2(norm_ffn[i]), wu[i], cw[i], cb[i], wd[i], shift=shift,
                               mixer_out=mixer_out)
            conv_new.append(st)
        return x, new, conv_new

    halo_p = SUBLANES
    rope_p = _rope_tables(jnp.arange(seq, dtype=jnp.int32))

    def attn_fox_prompt(qb, kb, vb, nck):
        return _fox_attn_prompt(qb, kb, vb, nck)

    yp, newp, convp = trunk(
        x_prompt, shift=1, start=0,
        pool_prev=jnp.zeros((bp, 2 * SUBLANES, D_MODEL), F32),
        conv_prev=[jnp.zeros((bp, 2 * FFN_NCH, halo_p, FFN_CK), F32)] * DEPTH,
        rope=rope_p,
        sgu_mix=sgu_w_s, sgu_bias=sgu_b_s.T,
        attn_diff=lambda qb, kb, vb, li: _diff_attn_prompt(qb, kb, vb, *lvec, onorm, li),
        attn_fox=attn_fox_prompt, emit_v=False)

    rows = dec_seq * nb
    t_of_row = jnp.arange(rows, dtype=jnp.int32) // nb
    rope_s = _rope_tables(past_len + t_of_row)
    fk_cache = cache_fox_k.transpose(0, 2, 3, 1)
    fv_cache = cache_fox_v.transpose(0, 2, 3, 1)
    fl_cache = cache_fox_logf.transpose(0, 2, 1)
    npad = 2 * SUBLANES

    def seq_major(a):
        return _from_time_major(a, nb)

    def pad_new(a):
        return jnp.pad(seq_major(a), ((0, 0), (0, npad - dec_seq), (0, 0)))

    def attn_diff_sample(qb, kb, vb, li):
        q = seq_major(qb).reshape(nb, dec_seq, DIFF_HEADS, LANES).transpose(0, 2, 1, 3)
        q = jnp.tile(q, (1, 1, 2, 1)).reshape(nb, DIFF_HEADS * 2 * dec_seq, LANES)
        flat_new = lambda a: pad_new(a).reshape(nb, npad * DIFF_HEADS, LANES)
        o = _diff_attn_sample(page_table, q, cache_diff_k, cache_diff_v, flat_new(kb), flat_new(vb),
                              *lvec, onorm, li, dec_seq)
        return _to_time_major(o)

    def attn_fox_sample(qb, kb, vb, nck):
        ncn = jnp.pad(jnp.swapaxes(seq_major(nck[..., :FOX_HEADS]), 1, 2), ((0, 0), (0, 0), (0, npad - dec_seq)))
        q = jnp.repeat(seq_major(qb), FOX_HEADS, axis=1)
        o = _fox_attn_sample(page_table, q, fk_cache, fv_cache, fl_cache, pad_new(kb), pad_new(vb), ncn,
                             dec_seq)
        return _to_time_major(o)

    halo_s = (FFN_CONV - 1) * nb
    conv_prev_s = [
        state_ffn_conv[i].reshape(nb, FFN_CONV - 1, 2 * FFN_NCH, FFN_CK).transpose(2, 1, 0, 3)
        .reshape(1, 2 * FFN_NCH, halo_s, FFN_CK) for i in range(DEPTH)]
    sgu_mix_s = jnp.repeat(jnp.repeat(sgu_w_s[:, :dec_seq, :dec_seq], nb, axis=1), nb, axis=2)
    sgu_bias_s = jnp.repeat(sgu_b_s[:, :dec_seq].T, nb, axis=0)
    ys, news, convs = trunk(
        _to_time_major(x_sample), shift=nb, start=past_len,
        pool_prev=_to_time_major(state_pool),
        conv_prev=conv_prev_s, rope=rope_s,
        sgu_mix=sgu_mix_s, sgu_bias=sgu_bias_s,
        attn_diff=attn_diff_sample, attn_fox=attn_fox_sample, emit_v=True)

    def conv_state_prompt(st):
        return st.transpose(0, 2, 1, 3).reshape(bp, FFN_CONV - 1, 2 * FFN_DIM)

    def conv_state_sample(st):
        return (st.reshape(2 * FFN_NCH, FFN_CONV - 1, nb, FFN_CK).transpose(2, 1, 0, 3)
                .reshape(nb, FFN_CONV - 1, 2 * FFN_DIM))

    hd4 = lambda a, h: a.reshape(a.shape[0], a.shape[1], h, D_MODEL // h)
    sm = seq_major
    return (
        yp, sm(ys),
        newp["pool"], sm(news["pool"]),
        hd4(newp["diff_k"], DIFF_HEADS), hd4(newp["diff_v"], DIFF_HEADS),
        hd4(sm(news["diff_k"]), DIFF_HEADS), hd4(sm(news["diff_v"]), DIFF_HEADS),
        hd4(newp["fox_k"], FOX_HEADS), hd4(newp["fox_v"], FOX_HEADS), newp["fox_logf"],
        hd4(sm(news["fox_k"]), FOX_HEADS), hd4(sm(news["fox_v"]), FOX_HEADS), sm(news["fox_logf"]),
        sm(news["sgu_v"]),
        jnp.stack([conv_state_prompt(s) for s in convp], axis=0),
        jnp.stack([conv_state_sample(s) for s in convs], axis=0),
    )
```

```python
import functools
import math

import jax
import jax.numpy as jnp
from jax import lax
from jax.experimental import pallas as pl
from jax.experimental.pallas import tpu as pltpu

F32 = jnp.float32
BF16 = jnp.bfloat16

D_MODEL = 1024
DEPTH = 4
PAGE_SIZE = 128
POOL_WINDOWS = (2, 4, 8, 16)
POOL_GD = D_MODEL // len(POOL_WINDOWS)
POOL_BUF = max(POOL_WINDOWS) - 1
DIFF_HEADS = 8
DIFF_HD = 64
FOX_HEADS = 16
FOX_HD = 64
SG_WIDTH = 2 * D_MODEL
SG_GROUPS = 8
SG_GD = SG_WIDTH // SG_GROUPS
SG_CHUNK = 128
FFN_DIM = 2816
FFN_CONV = 3
ROPE_THETA = 500000.0
ROPE_DIM = DIFF_HD // 4
NORM_EPS = 1e-6

LANES = 128
SUBLANES = 8
FFN_CK = 256
FFN_NCH = FFN_DIM // FFN_CK
SEG = 64
NSEG = D_MODEL // SEG
PAGES_PER_STEP = 8
ROW_TILE = 512
SGU_TILE = 256
ATT_TILE = 512
VMEM_LIMIT = 56 * 1024 * 1024
LOG2E = math.log2(math.e)


def _cparams(*sem):
    return pltpu.CompilerParams(dimension_semantics=sem, vmem_limit_bytes=VMEM_LIMIT)


def _rms_rows(x, g):
    ms = jnp.mean(x * x, axis=-1, keepdims=True)
    return x * lax.rsqrt(ms + NORM_EPS) * g


def _dot(a, b):
    return jnp.dot(a, b, preferred_element_type=F32)


def _dot_nt(a, b):
    return lax.dot_general(a, b, (((1,), (1,)), ((), ())), preferred_element_type=F32)


def _split3(a):
    hi = a.astype(BF16)
    r1 = a - hi.astype(F32)
    mid = r1.astype(BF16)
    lo = (r1 - mid.astype(F32)).astype(BF16)
    return hi, mid, lo


def _time_mask(n, shift):
    r = lax.broadcasted_iota(jnp.int32, (n, n), 0)
    c = lax.broadcasted_iota(jnp.int32, (n, n), 1)
    m = c <= r
    if shift > 1:
        m = jnp.logical_and(m, (r - c) % shift == 0)
    return m


def _const_spec(shape):
    nd = len(shape)
    return pl.BlockSpec(shape, lambda *_: (0,) * nd, pipeline_mode=pl.Buffered(1))


def _pool_kernel(x_ref, prev_ref, g_ref, w_ref, scale_ref, y_ref, st_ref, hext,
                 *, tq, shift, halo, start, nt):
    t = pl.program_id(1)

    @pl.when(t == 0)
    def _():
        hext[0:halo, :] = prev_ref[0]

    x = x_ref[0]
    h = _rms_rows(x, g_ref[...])
    hext[halo:halo + tq, :] = h
    row = lax.broadcasted_iota(jnp.int32, (tq, 1), 0)
    pos = start + (t * tq + row) // shift
    for g, win in enumerate(POOL_WINDOWS):
        c0 = g * POOL_GD
        hg = h[:, c0:c0 + POOL_GD]
        s = hg
        for j in range(1, win):
            o = halo - j * shift
            s = s + hext[o:o + tq, c0:c0 + POOL_GD]
        cnt = jnp.minimum(pos + 1, win).astype(F32)
        d = (s / cnt - hg).astype(BF16)
        y = _dot(d, w_ref[g])
        y_ref[0, :, c0:c0 + POOL_GD] = x[:, c0:c0 + POOL_GD] + y * scale_ref[:, c0:c0 + POOL_GD]

    n = POOL_BUF * shift

    @pl.when(t == nt - 1)
    def _():
        st_ref[0] = hext[halo + tq - n:halo + tq, :]

    if nt > 1:
        hext[0:halo, :] = hext[tq:tq + halo, :]


def _pool_layer(x, prev, g, w_bf, scale, *, shift, start):
    B, T, D = x.shape
    tq = min(T, ROW_TILE)
    nt = T // tq
    halo = prev.shape[1]
    n = POOL_BUF * shift
    kern = functools.partial(_pool_kernel, tq=tq, shift=shift, halo=halo, start=start, nt=nt)
    return pl.pallas_call(
        kern,
        grid=(B, nt),
        in_specs=[
            pl.BlockSpec((1, tq, D), lambda b, t: (b, t, 0)),
            pl.BlockSpec((1, halo, D), lambda b, t: (b, 0, 0)),
            _const_spec((1, D)),
            _const_spec(w_bf.shape),
            _const_spec((1, D)),
        ],
        out_specs=[
            pl.BlockSpec((1, tq, D), lambda b, t: (b, t, 0)),
            pl.BlockSpec((1, n, D), lambda b, t: (b, 0, 0)),
        ],
        out_shape=[jax.ShapeDtypeStruct((B, T, D), F32), jax.ShapeDtypeStruct((B, n, D), F32)],
        scratch_shapes=[pltpu.VMEM((halo + tq, D), F32)],
        compiler_params=_cparams("arbitrary", "arbitrary"),
        name="pool_mixer",
    )(x, prev, g, w_bf, scale)


def _ffn_kernel(x_ref, *rest, tq, shift, halo, nt, with_mixer_out):
    if with_mixer_out:
        o_ref, wo_ref = rest[:2]
        rest = rest[2:]
    prev_ref, g_ref, wu_ref, cw_ref, cb_ref, wd_ref, y_ref, st_ref, hb, buf, act = rest
    t = pl.program_id(1)
    x = x_ref[0]
    if with_mixer_out:
        x = x + _dot(o_ref[0], wo_ref[...])
    hb[...] = _rms_rows(x, g_ref[...]).astype(BF16)

    @pl.when(t == 0)
    def _():
        buf[:, 0:halo, :] = prev_ref[0]

    def conv(idx):
        u = _dot(hb[...], wu_ref[idx])
        buf[idx, halo:halo + tq, :] = u
        w = cw_ref[idx]
        o2 = halo - 2 * shift
        o1 = halo - shift
        out = (cb_ref[idx] + w[0:1] * buf[idx, o2:o2 + tq, :]
               + w[1:2] * buf[idx, o1:o1 + tq, :] + w[2:3] * u)
        buf[idx, 0:halo, :] = buf[idx, tq:tq + halo, :]
        return out

    for c in range(FFN_NCH):
        ag = conv(c)
        av = conv(FFN_NCH + c)
        act[:, c * FFN_CK:(c + 1) * FFN_CK] = (ag / (1.0 + jnp.exp(-ag)) * av).astype(BF16)
    y_ref[0] = x + _dot(act[...], wd_ref[...])

    @pl.when(t == nt - 1)
    def _():
        st_ref[0] = buf[:, halo - 2 * shift:halo, :]


def _ffn_layer(x, prev, g, wu, cw, cb, wd, *, shift, mixer_out=None):
    B, T, D = x.shape
    tq = min(T, ROW_TILE)
    nt = T // tq
    halo = prev.shape[2]
    ns = (FFN_CONV - 1) * shift
    kern = functools.partial(_ffn_kernel, tq=tq, shift=shift, halo=halo, nt=nt,
                             with_mixer_out=mixer_out is not None)
    row = pl.BlockSpec((1, tq, D), lambda b, t: (b, t, 0))
    extra_specs, extra_args = [], []
    if mixer_out is not None:
        extra_specs = [row, _const_spec(mixer_out[1].shape)]
        extra_args = list(mixer_out)
    return pl.pallas_call(
        kern,
        grid=(B, nt),
        in_specs=[row] + extra_specs + [
            pl.BlockSpec((1, 2 * FFN_NCH, halo, FFN_CK), lambda b, t: (b, 0, 0, 0)),
            _const_spec((1, D)),
            _const_spec(wu.shape),
            _const_spec(cw.shape),
            _const_spec(cb.shape),
            _const_spec(wd.shape),
        ],
        out_specs=[
            pl.BlockSpec((1, tq, D), lambda b, t: (b, t, 0)),
            pl.BlockSpec((1, 2 * FFN_NCH, ns, FFN_CK), lambda b, t: (b, 0, 0, 0)),
        ],
        out_shape=[jax.ShapeDtypeStruct((B, T, D), F32),
                   jax.ShapeDtypeStruct((B, 2 * FFN_NCH, ns, FFN_CK), F32)],
        scratch_shapes=[pltpu.VMEM((tq, D), BF16),
                        pltpu.VMEM((2 * FFN_NCH, halo + tq, FFN_CK), F32),
                        pltpu.VMEM((tq, FFN_DIM), BF16)],
        compiler_params=_cparams("arbitrary", "arbitrary"),
        name="conv_ffn",
    )(x, *extra_args, prev, g, wu, cw, cb, wd)


def _seg_norm(a, seg_ref, segt_ref, gain):
    ms = _dot((a * a).astype(BF16), seg_ref[...]) * (1.0 / SEG)
    r = lax.rsqrt(ms + NORM_EPS)
    hi = r.astype(BF16)
    lo = (r - hi.astype(F32)).astype(BF16)
    rf = _dot(jnp.concatenate([hi, lo], axis=1), segt_ref[...])
    return a * rf * gain


def _rope_slab(s, cos, sin_lo, sin_hi):
    return s * cos + pltpu.roll(s, LANES - ROPE_DIM // 2, 1) * sin_lo + pltpu.roll(s, ROPE_DIM // 2, 1) * sin_hi


def _diff_proj_kernel(x_ref, g_ref, w_ref, seg_ref, segt_ref, qg_ref, kg_ref, cos_ref, s1_ref, s2_ref,
                      qb_ref, kf_ref, kb_ref, vf_ref, vb_ref, *, scale):
    hb = _rms_rows(x_ref[0], g_ref[...]).astype(BF16)
    cos, s1, s2 = cos_ref[...], s1_ref[...], s2_ref[...]
    q = _seg_norm(_dot(hb, w_ref[:, 0:D_MODEL]), seg_ref, segt_ref, qg_ref[...])
    for c in range(D_MODEL // LANES):
        sl = slice(c * LANES, (c + 1) * LANES)
        qb_ref[0, :, sl] = (_rope_slab(q[:, sl], cos, s1, s2) * scale).astype(BF16)
    k = _seg_norm(_dot(hb, w_ref[:, D_MODEL:2 * D_MODEL]), seg_ref, segt_ref, kg_ref[...])
    for c in range(D_MODEL // LANES):
        sl = slice(c * LANES, (c + 1) * LANES)
        kr = _rope_slab(k[:, sl], cos, s1, s2)
        kf_ref[0, :, sl] = kr
        kb_ref[0, :, sl] = kr.astype(BF16)
    v = _dot(hb, w_ref[:, 2 * D_MODEL:3 * D_MODEL])
    vf_ref[0] = v
    vb_ref[0] = v.astype(BF16)


def _diff_proj_layer(x, g, w_bf, seg, segt, qg, kg, cos, s1, s2):
    B, T, D = x.shape
    tq = min(T, ROW_TILE)
    row = pl.BlockSpec((1, tq, D), lambda b, t: (b, t, 0))
    tab = pl.BlockSpec((tq, LANES), lambda b, t: (t, 0))
    kern = functools.partial(_diff_proj_kernel, scale=DIFF_HD ** -0.5 * LOG2E)
    return pl.pallas_call(
        kern,
        grid=(B, T // tq),
        in_specs=[row, _const_spec((1, D)), _const_spec(w_bf.shape), _const_spec(seg.shape),
                  _const_spec(segt.shape), _const_spec((1, D)), _const_spec((1, D)), tab, tab, tab],
        out_specs=[row, row, row, row, row],
        out_shape=[jax.ShapeDtypeStruct((B, T, D), BF16), jax.ShapeDtypeStruct((B, T, D), F32),
                   jax.ShapeDtypeStruct((B, T, D), BF16), jax.ShapeDtypeStruct((B, T, D), F32),
                   jax.ShapeDtypeStruct((B, T, D), BF16)],
        compiler_params=_cparams("arbitrary", "arbitrary"),
        name="diff_qkv_proj",
    )(x, g, w_bf, seg, segt, qg, kg, cos, s1, s2)


def _fox_proj_kernel(x_ref, g_ref, w_ref, wf_ref, bf_ref, seg_ref, segt_ref, qg_ref, kg_ref,
                     qb_ref, kf_ref, kb_ref, vf_ref, vb_ref, lf_ref, nck_ref, run,
                     *, tq, shift, scale):
    t = pl.program_id(1)

    @pl.when(t == 0)
    def _():
        run[...] = jnp.zeros_like(run)

    hb = _rms_rows(x_ref[0], g_ref[...]).astype(BF16)
    q = _seg_norm(_dot(hb, w_ref[:, 0:D_MODEL]), seg_ref, segt_ref, qg_ref[...])
    qb_ref[0] = (q * scale).astype(BF16)
    k = _seg_norm(_dot(hb, w_ref[:, D_MODEL:2 * D_MODEL]), seg_ref, segt_ref, kg_ref[...])
    kf_ref[0] = k
    kb_ref[0] = k.astype(BF16)
    v = _dot(hb, w_ref[:, 2 * D_MODEL:3 * D_MODEL])
    vf_ref[0] = v
    vb_ref[0] = v.astype(BF16)
    z = _dot(hb, wf_ref[...]) + bf_ref[...]
    lf = jnp.minimum(z, 0.0) - jnp.log1p(jnp.exp(-jnp.abs(z)))
    lf_ref[0] = lf
    m = jnp.where(_time_mask(tq, shift), 1.0, 0.0).astype(BF16)
    hi, mid, lo = _split3(lf)
    cum = _dot(m, hi) + _dot(m, mid) + _dot(m, lo) + run[...]
    nck_ref[0] = cum * (-LOG2E)
    run[...] = cum[tq - 1:tq, :]


def _fox_proj_layer(x, g, w_bf, wf_bf, bf, seg, segt, qg, kg, *, shift):
    B, T, D = x.shape
    tq = min(T, ROW_TILE)
    row = pl.BlockSpec((1, tq, D), lambda b, t: (b, t, 0))
    nar = pl.BlockSpec((1, tq, LANES), lambda b, t: (b, t, 0))
    kern = functools.partial(_fox_proj_kernel, tq=tq, shift=shift, scale=FOX_HD ** -0.5 * LOG2E)
    return pl.pallas_call(
        kern,
        grid=(B, T // tq),
        in_specs=[row, _const_spec((1, D)), _const_spec(w_bf.shape), _const_spec(wf_bf.shape),
                  _const_spec((1, LANES)), _const_spec(seg.shape), _const_spec(segt.shape),
                  _const_spec((1, D)), _const_spec((1, D))],
        out_specs=[row, row, row, row, row, nar, nar],
        out_shape=[jax.ShapeDtypeStruct((B, T, D), BF16), jax.ShapeDtypeStruct((B, T, D), F32),
                   jax.ShapeDtypeStruct((B, T, D), BF16), jax.ShapeDtypeStruct((B, T, D), F32),
                   jax.ShapeDtypeStruct((B, T, D), BF16), jax.ShapeDtypeStruct((B, T, LANES), F32),
                   jax.ShapeDtypeStruct((B, T, LANES), F32)],
        scratch_shapes=[pltpu.VMEM((1, LANES), F32)],
        compiler_params=_cparams("arbitrary", "arbitrary"),
        name="fox_qkvf_proj",
    )(x, g, w_bf, wf_bf, bf, seg, segt, qg, kg)


def _diff_lambda(lq1, lk1, lq2, lk2, lam_init):
    a = jnp.sum(lq1[...] * lk1[...], axis=-1, keepdims=True)
    b = jnp.sum(lq2[...] * lk2[...], axis=-1, keepdims=True)
    return jnp.exp(a) - jnp.exp(b) + lam_init


ATT_CHUNK = 32


def _softmax_tile_t(s_ref, p_ref, vt, m, l, acc, masked):
    tk, tq = s_ref.shape
    nch = tk // ATT_CHUNK

    def chunk(c):
        blk = s_ref[c * ATT_CHUNK:(c + 1) * ATT_CHUNK, :]
        if masked:
            key = lax.broadcasted_iota(jnp.int32, (ATT_CHUNK, tq), 0) + c * ATT_CHUNK
            qry = lax.broadcasted_iota(jnp.int32, (ATT_CHUNK, tq), 1)
            blk = jnp.where(key <= qry, blk, -jnp.inf)
        return blk

    def fold(x, op):
        return op(x.reshape(ATT_CHUNK // SUBLANES, SUBLANES, tq), axis=0)

    mx = fold(chunk(0), jnp.max)
    for c in range(1, nch):
        mx = jnp.maximum(mx, fold(chunk(c), jnp.max))
    m_new = jnp.maximum(m, jnp.max(mx, axis=0, keepdims=True))
    alpha = jnp.exp2(m - m_new)
    ls = None
    for c in range(nch):
        p = jnp.exp2(chunk(c) - m_new)
        ps = fold(p, jnp.sum)
        ls = ps if ls is None else ls + ps
        p_ref[c * ATT_CHUNK:(c + 1) * ATT_CHUNK, :] = p.astype(BF16)
    l = alpha * l + jnp.sum(ls, axis=0, keepdims=True)
    acc = alpha * acc + _dot(vt, p_ref[...])
    return m_new, l, acc


def _stage_vt(v_ref, vt_sc, tq):
    for j in range(vt_sc.shape[0]):
        vt_sc[j] = v_ref[0, j * tq:(j + 1) * tq, :].astype(F32).T.astype(BF16)


def _flash_tiles(qi, put, consume, init, slot_a, slot_b):
    put(0, slot_a)

    def pair(a, stats):
        put(2 * a + 1, slot_b)
        stats = consume(2 * a, slot_a, stats, False)
        put(2 * a + 2, slot_a)
        return consume(2 * a + 1, slot_b, stats, False)

    stats = lax.fori_loop(0, qi // 2, pair, init)

    def last_in_a():
        return consume(qi, slot_a, stats, True)

    def last_in_b():
        put(qi, slot_b)
        return consume(qi, slot_b, consume(qi - 1, slot_a, stats, False), True)

    return lax.cond(qi % 2 == 0, last_in_a, last_in_b)


def _diff_attn_kernel(q_ref, k_ref, v_ref, lq1, lk1, lq2, lk2, on_ref, o_ref, vt_sc, sa_sc, sb_sc, p_sc,
                      *, tq, lam_init):
    @pl.when(pl.program_id(0) >= 0)
    def _():
        _stage_vt(v_ref, vt_sc, tq)

    lo = lax.broadcasted_iota(jnp.int32, (1, LANES), 1) < DIFF_HD

    def query_tile(qi, _):
        rows = pl.ds(pl.multiple_of(qi * tq, tq), tq)
        q = q_ref[0, rows, :]
        zero = jnp.zeros_like(q)
        qs = (jnp.where(lo, q, zero), jnp.where(lo, zero, q))

        def put(kj, slot):
            k = k_ref[0, pl.ds(pl.multiple_of(kj * tq, tq), tq), :]
            for i in range(2):
                slot[i] = _dot_nt(k, qs[i])

        def consume(kj, slot, stats, masked):
            vt = vt_sc[kj]
            out = []
            for i in range(2):
                out.extend(_softmax_tile_t(slot.at[i], p_sc.at[i], vt, *stats[3 * i:3 * i + 3], masked))
            return tuple(out)

        init = (jnp.full((1, tq), -jnp.inf, F32), jnp.zeros((1, tq), F32), jnp.zeros((LANES, tq), F32)) * 2
        m1, l1, a1, m2, l2, a2 = _flash_tiles(qi, put, consume, init, sa_sc, sb_sc)
        lam = _diff_lambda(lq1, lk1, lq2, lk2, lam_init)
        o = (a1 / l1 - lam * (a2 / l2)).T
        o_ref[0, rows, :] = (_rms_rows(o, on_ref[...]) * (1.0 - lam_init)).astype(BF16)
        return 0

    lax.fori_loop(0, vt_sc.shape[0], query_tile, 0)


def _attn_scratch(T, tq):
    return [pltpu.VMEM((T // tq, LANES, tq), BF16), pltpu.VMEM((2, tq, tq), F32),
            pltpu.VMEM((2, tq, tq), F32), pltpu.VMEM((2, tq, tq), BF16)]


def _diff_attn_prompt(qb, kb, vb, lq1, lk1, lq2, lk2, onorm, lam_init):
    B, T, D = qb.shape
    tq = ATT_TILE
    kern = functools.partial(_diff_attn_kernel, tq=tq, lam_init=lam_init)
    vec = _const_spec((1, DIFF_HD))
    head = lambda: pl.BlockSpec((1, T, LANES), lambda b, h: (b, 0, h))
    return pl.pallas_call(
        kern,
        grid=(B, DIFF_HEADS),
        in_specs=[head(), head(), head(), vec, vec, vec, vec, _const_spec((1, LANES))],
        out_specs=head(),
        out_shape=jax.ShapeDtypeStruct((B, T, D), BF16),
        scratch_shapes=_attn_scratch(T, tq),
        compiler_params=_cparams("arbitrary", "arbitrary"),
        name="diff_attn_prompt",
    )(qb, kb, vb, lq1, lk1, lq2, lk2, onorm)


def _fox_attn_kernel(q_ref, k_ref, v_ref, nck_ref, o_ref, vt_sc, sa_sc, sb_sc, p_sc, nck_sc, *, tq):
    @pl.when(pl.program_id(0) >= 0)
    def _():
        _stage_vt(v_ref, vt_sc, tq)
        lane = lax.broadcasted_iota(jnp.int32, (1, LANES), 1)
        for i in range(2):
            own = lane == 2 * pl.program_id(1) + i
            for j in range(vt_sc.shape[0]):
                rows = slice(j * tq, (j + 1) * tq)
                col = jnp.sum(jnp.where(own, nck_ref[0, rows, :], 0.0), axis=-1, keepdims=True)
                nck_sc[i, rows, :] = jnp.broadcast_to(col, (tq, LANES))

    lo = lax.broadcasted_iota(jnp.int32, (1, LANES), 1) < FOX_HD

    def query_tile(qi, _):
        rows = pl.ds(pl.multiple_of(qi * tq, tq), tq)
        q = q_ref[0, rows, :]
        zero = jnp.zeros_like(q)
        qs = (jnp.where(lo, q, zero), jnp.where(lo, zero, q))

        def put(kj, slot):
            off = pl.multiple_of(kj * tq, tq)
            k = k_ref[0, pl.ds(off, tq), :]
            for i in range(2):
                nck = nck_sc[i, pl.ds(off, tq), :]
                slot[i] = _dot_nt(k, qs[i]) + jnp.concatenate([nck] * (tq // LANES), axis=1)

        def consume(kj, slot, stats, masked):
            vt = vt_sc[kj]
            out = []
            for i in range(2):
                out.extend(_softmax_tile_t(slot.at[i], p_sc.at[i], vt[i * FOX_HD:(i + 1) * FOX_HD],
                                           *stats[3 * i:3 * i + 3], masked))
            return tuple(out)

        init = (jnp.full((1, tq), -jnp.inf, F32), jnp.zeros((1, tq), F32), jnp.zeros((FOX_HD, tq), F32)) * 2
        m1, l1, a1, m2, l2, a2 = _flash_tiles(qi, put, consume, init, sa_sc, sb_sc)
        o_ref[0, rows, :] = jnp.concatenate([a1 / l1, a2 / l2], axis=0).T.astype(BF16)
        return 0

    lax.fori_loop(0, vt_sc.shape[0], query_tile, 0)


def _fox_attn_prompt(qb, kb, vb, nck):
    B, T, D = qb.shape
    tq = ATT_TILE
    kern = functools.partial(_fox_attn_kernel, tq=tq)
    head = lambda: pl.BlockSpec((1, T, LANES), lambda b, h: (b, 0, h))
    return pl.pallas_call(
        kern,
        grid=(B, FOX_HEADS // 2),
        in_specs=[head(), head(), head(), pl.BlockSpec((1, T, LANES), lambda b, h: (b, 0, 0))],
        out_specs=head(),
        out_shape=jax.ShapeDtypeStruct((B, T, D), BF16),
        scratch_shapes=_attn_scratch(T, tq) + [pltpu.VMEM((2, T, LANES), F32)],
        compiler_params=_cparams("arbitrary", "arbitrary"),
        name="fox_attn_prompt",
    )(qb, kb, vb, nck)


def _page_specs(n_pages_step, page_shape):
    P = n_pages_step
    zeros = (0,) * len(page_shape)

    def mk(p):
        return pl.BlockSpec((1,) + tuple(page_shape), lambda b, j, pt: (pt[b, j * P + p],) + zeros)

    return [mk(p) for p in range(P)]


def _rows_softmax(s, m_sc, l_sc):
    m = m_sc[...]
    m_new = jnp.maximum(m, jnp.max(s, axis=-1, keepdims=True))
    alpha = jnp.exp2(m - m_new)
    p = jnp.exp2(s - m_new)
    l_sc[...] = alpha * l_sc[...] + jnp.sum(p, axis=-1, keepdims=True)
    m_sc[...] = m_new
    return alpha, p.astype(BF16)


def _diff_sample_kernel(pt_ref, q_ref, *rest, P, nj, dec_seq, lam_init):
    kpages = rest[:P]
    vpages = rest[P:2 * P]
    kn_ref, vn_ref, lq1, lk1, lq2, lk2, on_ref, o_ref, qm_sc, mask_sc, m_sc, l_sc, acc_sc = rest[2 * P:]
    j = pl.program_id(1)
    nr = 2 * dec_seq
    nrow = DIFF_HEADS * nr
    ncol = PAGE_SIZE * DIFF_HEADS

    @pl.when(j == 0)
    def _():
        r = lax.broadcasted_iota(jnp.int32, (nrow, LANES), 0)
        c = lax.broadcasted_iota(jnp.int32, (nrow, LANES), 1)
        q = q_ref[0]
        qm_sc[...] = jnp.where(c // DIFF_HD == (r % nr) // dec_seq, q, jnp.zeros_like(q))
        rr = lax.broadcasted_iota(jnp.int32, (nrow, ncol), 0)
        cc = lax.broadcasted_iota(jnp.int32, (nrow, ncol), 1)
        mask_sc[...] = jnp.where(cc % DIFF_HEADS == rr // nr, 0.0, -jnp.inf)
        m_sc[...] = jnp.full_like(m_sc, -jnp.inf)
        l_sc[...] = jnp.zeros_like(l_sc)
        acc_sc[...] = jnp.zeros_like(acc_sc)

    def flat(pg):
        return pg[0].reshape(ncol, LANES).astype(BF16)

    qm = qm_sc[...]
    ss = [_dot_nt(qm, flat(pg)) + mask_sc[...] for pg in kpages]
    m = m_sc[...]
    m_new = m
    for s in ss:
        m_new = jnp.maximum(m_new, jnp.max(s, axis=-1, keepdims=True))
    alpha = jnp.exp2(m - m_new)
    l = alpha * l_sc[...]
    acc = alpha * acc_sc[...]
    for s, pg in zip(ss, vpages):
        p = jnp.exp2(s - m_new)
        l = l + jnp.sum(p, axis=-1, keepdims=True)
        acc = acc + _dot(p.astype(BF16), flat(pg))
    m_sc[...] = m_new
    l_sc[...] = l
    acc_sc[...] = acc

    @pl.when(j == nj - 1)
    def _():
        npad = kn_ref.shape[1]
        rr = lax.broadcasted_iota(jnp.int32, (nrow, npad), 0)
        cc = lax.broadcasted_iota(jnp.int32, (nrow, npad), 1)
        allowed = jnp.logical_and(cc % DIFF_HEADS == rr // nr, cc // DIFF_HEADS <= rr % dec_seq)
        sn = jnp.where(allowed, _dot_nt(qm, kn_ref[0]), -jnp.inf)
        alpha_n, pn = _rows_softmax(sn, m_sc, l_sc)
        o = (alpha_n * acc + _dot(pn, vn_ref[0])) / l_sc[...]
        lam = _diff_lambda(lq1, lk1, lq2, lk2, lam_init)
        for h in range(DIFF_HEADS):
            od = o[h * nr:h * nr + dec_seq] - lam * o[h * nr + dec_seq:(h + 1) * nr]
            o_ref[0, :, h * LANES:(h + 1) * LANES] = (
                _rms_rows(od, on_ref[...]) * (1.0 - lam_init)).astype(BF16)


def _diff_attn_sample(page_table, q, cache_k, cache_v, kn, vn, lq1, lk1, lq2, lk2, onorm, lam_init,
                      dec_seq):
    nb, npages = page_table.shape
    P = PAGES_PER_STEP
    nj = npages // P
    nrow = q.shape[1]
    kern = functools.partial(_diff_sample_kernel, P=P, nj=nj, dec_seq=dec_seq, lam_init=lam_init)
    per_seq = lambda a: pl.BlockSpec((1,) + a.shape[1:], lambda b, j, pt: (b,) + (0,) * (a.ndim - 1))
    vec = pl.BlockSpec((1, DIFF_HD), lambda b, j, pt: (0, 0))
    page = cache_k.shape[1:]
    grid_spec = pltpu.PrefetchScalarGridSpec(
        num_scalar_prefetch=1,
        grid=(nb, nj),
        in_specs=([per_seq(q)] + _page_specs(P, page) + _page_specs(P, page)
                  + [per_seq(kn), per_seq(vn), vec, vec, vec, vec,
                     pl.BlockSpec((1, LANES), lambda b, j, pt: (0, 0))]),
        out_specs=pl.BlockSpec((1, dec_seq, D_MODEL), lambda b, j, pt: (b, 0, 0)),
        scratch_shapes=[pltpu.VMEM((nrow, LANES), BF16), pltpu.VMEM((nrow, PAGE_SIZE * DIFF_HEADS), F32),
                        pltpu.VMEM((nrow, 1), F32), pltpu.VMEM((nrow, 1), F32),
                        pltpu.VMEM((nrow, LANES), F32)],
    )
    return pl.pallas_call(
        kern,
        grid_spec=grid_spec,
        out_shape=jax.ShapeDtypeStruct((nb, dec_seq, D_MODEL), BF16),
        compiler_params=_cparams("arbitrary", "arbitrary"),
        name="diff_attn_sample",
    )(page_table, q, *([cache_k] * P), *([cache_v] * P), kn, vn, lq1, lk1, lq2, lk2, onorm)


def _fox_sample_kernel(pt_ref, q_ref, *rest, P, nj, dec_seq):
    kpages = rest[:P]
    vpages = rest[P:2 * P]
    lpages = rest[2 * P:3 * P]
    kn_ref, vn_ref, ncn_ref, o_ref, qt_sc, run_sc, m_sc, l_sc, acc_sc = rest[3 * P:]
    j = pl.program_id(1)
    nrow = dec_seq * FOX_HEADS

    @pl.when(j == 0)
    def _():
        rq = lax.broadcasted_iota(jnp.int32, (nrow, D_MODEL), 0)
        cq = lax.broadcasted_iota(jnp.int32, (nrow, D_MODEL), 1)
        q = q_ref[0]
        qt_sc[...] = jnp.where(cq // FOX_HD == rq % FOX_HEADS, q, jnp.zeros_like(q))
        run_sc[...] = jnp.zeros_like(run_sc)
        m_sc[...] = jnp.full_like(m_sc, -jnp.inf)
        l_sc[...] = jnp.zeros_like(l_sc)
        acc_sc[...] = jnp.zeros_like(acc_sc)

    r = lax.broadcasted_iota(jnp.int32, (PAGE_SIZE, PAGE_SIZE), 0)
    c = lax.broadcasted_iota(jnp.int32, (PAGE_SIZE, PAGE_SIZE), 1)
    upper = jnp.where(r <= c, 1.0, 0.0).astype(BF16)
    run = run_sc[...]
    cums = []
    for pg in lpages:
        hi, mid, lo = _split3(pg[0])
        local = _dot(hi, upper) + _dot(mid, upper) + _dot(lo, upper)
        cums.append(local + run)
        run = run + local[:, PAGE_SIZE - 1:PAGE_SIZE]
    run_sc[...] = run
    nck = jnp.concatenate(cums, axis=1) * (-LOG2E)

    def flat_t(pages):
        return jnp.concatenate([pg[0].reshape(D_MODEL, PAGE_SIZE).astype(BF16) for pg in pages], axis=1)

    qt = qt_sc[...]
    s = _dot(qt, flat_t(kpages)) + jnp.concatenate([nck] * dec_seq, axis=0)
    alpha, p = _rows_softmax(s, m_sc, l_sc)
    acc = alpha * acc_sc[...] + _dot_nt(p, flat_t(vpages))
    acc_sc[...] = acc

    @pl.when(j == nj - 1)
    def _():
        npad = kn_ref.shape[1]
        ncn = run * (-LOG2E) + ncn_ref[0]
        sn = _dot_nt(qt, kn_ref[0]) + jnp.concatenate([ncn] * dec_seq, axis=0)
        rr = lax.broadcasted_iota(jnp.int32, (nrow, npad), 0)
        cc = lax.broadcasted_iota(jnp.int32, (nrow, npad), 1)
        sn = jnp.where(cc <= rr // FOX_HEADS, sn, -jnp.inf)
        alpha_n, pn = _rows_softmax(sn, m_sc, l_sc)
        o = (alpha_n * acc + _dot(pn, vn_ref[0])) / l_sc[...]
        ro = lax.broadcasted_iota(jnp.int32, (nrow, D_MODEL), 0)
        co = lax.broadcasted_iota(jnp.int32, (nrow, D_MODEL), 1)
        o = jnp.where(co // FOX_HD == ro % FOX_HEADS, o, 0.0)
        rows = [jnp.sum(o[t * FOX_HEADS:(t + 1) * FOX_HEADS], axis=0, keepdims=True)
                for t in range(dec_seq)]
        o_ref[0] = jnp.concatenate(rows, axis=0).astype(BF16)


def _fox_attn_sample(page_table, q, cache_kt, cache_vt, cache_lt, kn, vn, ncn, dec_seq):
    nb, npages = page_table.shape
    P = PAGES_PER_STEP
    nj = npages // P
    nrow = q.shape[1]
    kern = functools.partial(_fox_sample_kernel, P=P, nj=nj, dec_seq=dec_seq)
    per_seq = lambda a: pl.BlockSpec((1,) + a.shape[1:], lambda b, j, pt: (b,) + (0,) * (a.ndim - 1))
    grid_spec = pltpu.PrefetchScalarGridSpec(
        num_scalar_prefetch=1,
        grid=(nb, nj),
        in_specs=([per_seq(q)] + _page_specs(P, cache_kt.shape[1:]) + _page_specs(P, cache_vt.shape[1:])
                  + _page_specs(P, cache_lt.shape[1:]) + [per_seq(kn), per_seq(vn), per_seq(ncn)]),
        out_specs=pl.BlockSpec((1, dec_seq, D_MODEL), lambda b, j, pt: (b, 0, 0)),
        scratch_shapes=[pltpu.VMEM((nrow, D_MODEL), BF16), pltpu.VMEM((FOX_HEADS, 1), F32),
                        pltpu.VMEM((nrow, 1), F32), pltpu.VMEM((nrow, 1), F32),
                        pltpu.VMEM((nrow, D_MODEL), F32)],
    )
    return pl.pallas_call(
        kern,
        grid_spec=grid_spec,
        out_shape=jax.ShapeDtypeStruct((nb, dec_seq, D_MODEL), BF16),
        compiler_params=_cparams("arbitrary", "arbitrary"),
        name="fox_attn_sample",
    )(page_table, q, *([cache_kt] * P), *([cache_vt] * P), *([cache_lt] * P), kn, vn, ncn)


def _sgu_kernel(x_ref, g_ref, win_ref, vn_ref, wmix_ref, bias_ref, wout_ref, *rest,
                tq, shift, emit_v):
    if emit_v:
        y_ref, v_ref, um = rest
    else:
        y_ref, um = rest
    x = x_ref[0]
    hb = _rms_rows(x, g_ref[...]).astype(BF16)

    def gelu(a):
        return 0.5 * a * (1.0 + lax.erf(a * (1.0 / math.sqrt(2.0))))

    v = _rms_rows(gelu(_dot(hb, win_ref[:, SG_WIDTH:2 * SG_WIDTH])), vn_ref[...])
    if emit_v:
        v_ref[0] = v
    vb = v.astype(BF16)
    u = gelu(_dot(hb, win_ref[:, 0:SG_WIDTH]))
    mask = _time_mask(SG_CHUNK, shift)
    for g in range(SG_GROUPS):
        w = jnp.where(mask, wmix_ref[g], 0.0).astype(BF16)
        cs = slice(g * SG_GD, (g + 1) * SG_GD)
        for ch in range(tq // SG_CHUNK):
            rs = slice(ch * SG_CHUNK, (ch + 1) * SG_CHUNK)
            mixed = _dot(w, vb[rs, cs]) + bias_ref[:, g:g + 1]
            um[rs, cs] = (u[rs, cs] * mixed).astype(BF16)
    y_ref[0] = x + _dot(um[...], wout_ref[...])


def _sgu_layer(x, g, win_bf, vnorm, wmix, bias, wout_bf, *, shift, emit_v):
    B, T, D = x.shape
    tq = min(T, SGU_TILE)
    row = pl.BlockSpec((1, tq, D), lambda b, t: (b, t, 0))
    out_specs = [row]
    out_shape = [jax.ShapeDtypeStruct((B, T, D), F32)]
    if emit_v:
        out_specs.append(pl.BlockSpec((1, tq, SG_WIDTH), lambda b, t: (b, t, 0)))
        out_shape.append(jax.ShapeDtypeStruct((B, T, SG_WIDTH), F32))
    kern = functools.partial(_sgu_kernel, tq=tq, shift=shift, emit_v=emit_v)
    return pl.pallas_call(
        kern,
        grid=(B, T // tq),
        in_specs=[row, _const_spec((1, D)), _const_spec(win_bf.shape), _const_spec((1, SG_WIDTH)),
                  _const_spec(wmix.shape), _const_spec(bias.shape), _const_spec(wout_bf.shape)],
        out_specs=out_specs,
        out_shape=out_shape,
        scratch_shapes=[pltpu.VMEM((tq, SG_WIDTH), BF16)],
        compiler_params=_cparams("arbitrary", "arbitrary"),
        name="sgu_mixer",
    )(x, g, win_bf, vnorm, wmix, bias, wout_bf)


def _lambda_init(layer_idx):
    return 0.8 - 0.6 * math.exp(-0.3 * layer_idx)


def _rope_tables(pos):
    half = ROPE_DIM // 2
    inv = ROPE_THETA ** (-jnp.arange(half, dtype=F32) / half)
    ang = pos.astype(F32)[:, None] * inv[None, :]
    lane = jnp.arange(LANES) % SEG
    idx = lane % half
    cos = jnp.where(lane[None, :] < ROPE_DIM, jnp.cos(ang)[:, idx], 1.0)
    sin = jnp.sin(ang)[:, idx]
    s1 = jnp.where(lane[None, :] < half, -sin, 0.0)
    s2 = jnp.where(jnp.logical_and(lane[None, :] >= half, lane[None, :] < ROPE_DIM), sin, 0.0)
    return cos.astype(F32), s1.astype(F32), s2.astype(F32)


def _to_time_major(a):
    a = jnp.swapaxes(a, 0, 1)
    return a.reshape(1, a.shape[0] * a.shape[1], *a.shape[2:])


def _from_time_major(a, nb):
    a = a.reshape(a.shape[1] // nb, nb, *a.shape[2:])
    return jnp.swapaxes(a, 0, 1)


def kernel(x_prompt, x_sample, state_pool, cache_diff_k, cache_diff_v, cache_fox_k, cache_fox_v,
           cache_fox_logf, state_ffn_conv, page_table, norm_mix, norm_ffn, pool_w, pool_scale,
           diff_w_qkv, diff_q_norm, diff_k_norm, diff_lq1, diff_lk1, diff_lq2, diff_lk2, diff_o_norm,
           diff_w_o, fox_w_qkvf, fox_b_f, fox_q_norm, fox_k_norm, fox_w_o, sgu_w_in, sgu_v_norm,
           sgu_w_s, sgu_b_s, sgu_w_out, ffn_w_up, ffn_conv_w, ffn_conv_b, ffn_w_down):
    bp, seq, _ = x_prompt.shape
    nb, dec_seq, _ = x_sample.shape
    n_pool = cache_diff_k.shape[0]
    past_len = page_table.shape[1] * PAGE_SIZE
    row2 = lambda a: a.reshape(1, -1).astype(F32)

    pool_w_bf = pool_w.astype(BF16)
    wu = ffn_w_up.astype(BF16).reshape(DEPTH, D_MODEL, 2 * FFN_NCH, FFN_CK).transpose(0, 2, 1, 3)
    wd = ffn_w_down.astype(BF16)
    cw = ffn_conv_w.reshape(DEPTH, FFN_CONV, 2 * FFN_NCH, FFN_CK).transpose(0, 2, 1, 3)
    cb = ffn_conv_b.reshape(DEPTH, 2 * FFN_NCH, 1, FFN_CK)
    seg_id = jnp.arange(D_MODEL) // SEG
    seg = (seg_id[:, None] == jnp.arange(LANES)[None, :]).astype(BF16)
    segt = jnp.concatenate([seg.T, seg.T], axis=0)
    diff_w_bf = diff_w_qkv.astype(BF16)
    diff_wo_bf = diff_w_o.astype(BF16)
    fox_w_bf = fox_w_qkvf[:, :3 * D_MODEL].astype(BF16)
    fox_wf_bf = jnp.pad(fox_w_qkvf[:, 3 * D_MODEL:], ((0, 0), (0, LANES - FOX_HEADS))).astype(BF16)
    fox_bf = jnp.pad(fox_b_f, (0, LANES - FOX_HEADS)).reshape(1, LANES)
    fox_wo_bf = fox_w_o.astype(BF16)
    tile_seg = lambda g: jnp.tile(g, NSEG).reshape(1, D_MODEL)
    sgu_win_bf = sgu_w_in.astype(BF16)
    sgu_wout_bf = sgu_w_out.astype(BF16)
    lvec = [row2(a) for a in (diff_lq1, diff_lk1, diff_lq2, diff_lk2)]
    onorm = row2(diff_o_norm)

    def trunk(x, *, shift, start, pool_prev, conv_prev, rope, sgu_mix, sgu_bias, attn_diff, attn_fox, emit_v):
        new = {}
        conv_new = []
        for i in range(DEPTH):
            g = row2(norm_mix[i])
            mixer_out = None
            if i == 0:
                x, new["pool"] = _pool_layer(x, pool_prev, g, pool_w_bf, row2(pool_scale),
                                             shift=shift, start=start)
            elif i == 1:
                qb, kf, kb, vf, vb = _diff_proj_layer(x, g, diff_w_bf, seg, segt, tile_seg(diff_q_norm),
                                                      tile_seg(diff_k_norm), *rope)
                new["diff_k"], new["diff_v"] = kf, vf
                mixer_out = (attn_diff(qb, kb, vb, _lambda_init(i)), diff_wo_bf)
            elif i == 2:
                qb, kf, kb, vf, vb, lf, nck = _fox_proj_layer(
                    x, g, fox_w_bf, fox_wf_bf, fox_bf, seg, segt, tile_seg(fox_q_norm),
                    tile_seg(fox_k_norm), shift=shift)
                new["fox_k"], new["fox_v"], new["fox_logf"] = kf, vf, lf[..., :FOX_HEADS]
                mixer_out = (attn_fox(qb, kb, vb, nck), fox_wo_bf)
            else:
                res = _sgu_layer(x, g, sgu_win_bf, row2(sgu_v_norm), sgu_mix, sgu_bias, sgu_wout_bf,
                                 shift=shift, emit_v=emit_v)
                x = res[0]
                if emit_v:
                    new["sgu_v"] = res[1]
            x, st = _ffn_layer(x, conv_prev[i], row2(norm_ffn[i]), wu[i], cw[i], cb[i], wd[i], shift=shift,
                               mixer_out=mixer_out)
            conv_new.append(st)
        return x, new, conv_new

    halo_p = SUBLANES
    rope_p = _rope_tables(jnp.arange(seq, dtype=jnp.int32))

    def attn_fox_prompt(qb, kb, vb, nck):
        return _fox_attn_prompt(qb, kb, vb, nck)

    yp, newp, convp = trunk(
        x_prompt, shift=1, start=0,
        pool_prev=jnp.zeros((bp, 2 * SUBLANES, D_MODEL), F32),
        conv_prev=[jnp.zeros((bp, 2 * FFN_NCH, halo_p, FFN_CK), F32)] * DEPTH,
        rope=rope_p,
        sgu_mix=sgu_w_s, sgu_bias=sgu_b_s.T,
        attn_diff=lambda qb, kb, vb, li: _diff_attn_prompt(qb, kb, vb, *lvec, onorm, li),
        attn_fox=attn_fox_prompt, emit_v=False)

    rows = dec_seq * nb
    t_of_row = jnp.arange(rows, dtype=jnp.int32) // nb
    rope_s = _rope_tables(past_len + t_of_row)
    fk_cache = cache_fox_k.transpose(0, 2, 3, 1)
    fv_cache = cache_fox_v.transpose(0, 2, 3, 1)
    fl_cache = cache_fox_logf.transpose(0, 2, 1)
    npad = 2 * SUBLANES

    def seq_major(a):
        return _from_time_major(a, nb)

    def pad_new(a):
        return jnp.pad(seq_major(a), ((0, 0), (0, npad - dec_seq), (0, 0)))

    def attn_diff_sample(qb, kb, vb, li):
        q = seq_major(qb).reshape(nb, dec_seq, DIFF_HEADS, LANES).transpose(0, 2, 1, 3)
        q = jnp.tile(q, (1, 1, 2, 1)).reshape(nb, DIFF_HEADS * 2 * dec_seq, LANES)
        flat_new = lambda a: pad_new(a).reshape(nb, npad * DIFF_HEADS, LANES)
        o = _diff_attn_sample(page_table, q, cache_diff_k, cache_diff_v, flat_new(kb), flat_new(vb),
                              *lvec, onorm, li, dec_seq)
        return _to_time_major(o)

    def attn_fox_sample(qb, kb, vb, nck):
        ncn = jnp.pad(jnp.swapaxes(seq_major(nck[..., :FOX_HEADS]), 1, 2), ((0, 0), (0, 0), (0, npad - dec_seq)))
        q = jnp.repeat(seq_major(qb), FOX_HEADS, axis=1)
        o = _fox_attn_sample(page_table, q, fk_cache, fv_cache, fl_cache, pad_new(kb), pad_new(vb), ncn,
                             dec_seq)
        return _to_time_major(o)

    halo_s = (FFN_CONV - 1) * nb
    conv_prev_s = [
        state_ffn_conv[i].reshape(nb, FFN_CONV - 1, 2 * FFN_NCH, FFN_CK).transpose(2, 1, 0, 3)
        .reshape(1, 2 * FFN_NCH, halo_s, FFN_CK) for i in range(DEPTH)]
    sgu_mix_s = jnp.repeat(jnp.repeat(sgu_w_s[:, :dec_seq, :dec_seq], nb, axis=1), nb, axis=2)
    sgu_bias_s = jnp.repeat(sgu_b_s[:, :dec_seq].T, nb, axis=0)
    ys, news, convs = trunk(
        _to_time_major(x_sample), shift=nb, start=past_len,
        pool_prev=_to_time_major(state_pool),
        conv_prev=conv_prev_s, rope=rope_s,
        sgu_mix=sgu_mix_s, sgu_bias=sgu_bias_s,
        attn_diff=attn_diff_sample, attn_fox=attn_fox_sample, emit_v=True)

    def conv_state_prompt(st):
        return st.transpose(0, 2, 1, 3).reshape(bp, FFN_CONV - 1, 2 * FFN_DIM)

    def conv_state_sample(st):
        return (st.reshape(2 * FFN_NCH, FFN_CONV - 1, nb, FFN_CK).transpose(2, 1, 0, 3)
                .reshape(nb, FFN_CONV - 1, 2 * FFN_DIM))

    hd4 = lambda a, h: a.reshape(a.shape[0], a.shape[1], h, D_MODEL // h)
    sm = seq_major
    return (
        yp, sm(ys),
        newp["pool"], sm(news["pool"]),
        hd4(newp["diff_k"], DIFF_HEADS), hd4(newp["diff_v"], DIFF_HEADS),
        hd4(sm(news["diff_k"]), DIFF_HEADS), hd4(sm(news["diff_v"]), DIFF_HEADS),
        hd4(newp["fox_k"], FOX_HEADS), hd4(newp["fox_v"], FOX_HEADS), newp["fox_logf"],
        hd4(sm(news["fox_k"]), FOX_HEADS), hd4(sm(news["fox_v"]), FOX_HEADS), sm(news["fox_logf"]),
        sm(news["sgu_v"]),
        jnp.stack([conv_state_prompt(s) for s in convp], axis=0),
        jnp.stack([conv_state_sample(s) for s in convs], axis=0),
    )
```
